```python
import math
import jax, jax.numpy as jnp
from jax import lax
import numpy as np

D_MODEL = 4096
BATCH = 4
SEQ = 4096
DEPTH = 4
DEC_BATCH = 1
DEC_SEQ = 16384
PAST_LEN = 128

N_MIXERS = 2
N_A = (DEPTH + 1) // 2
N_B = DEPTH // 2
SC_WIDTH = 3
D_INNER = 2 * D_MODEL
SSM_HEAD_DIM = 64
SSM_HEADS = D_INNER // SSM_HEAD_DIM
SSM_GROUPS = 8
HEADS_PER_GROUP = SSM_HEADS // SSM_GROUPS
D_STATE = 128
SSM_CONV_WIDTH = 3
CONV_DIM = D_INNER + 2 * SSM_GROUPS * D_STATE
SSM_IN_DIM = D_INNER + CONV_DIM + 2 * SSM_HEADS
CHUNK = 128
SSM_NORM_EPS = 1e-5
N_MEM = 256
XA_HEADS = 4
XA_HEAD_DIM = 128
XA_WIDTH = XA_HEADS * XA_HEAD_DIM
D_FF = ((8 * D_MODEL // 3 + 255) // 256) * 256
FFN_CONV_WIDTH = 3
NORM_EPS = 1e-6

kernel_name = "hybrid_shortconv_ssd_bidir_encoder"


def rmsnorm(x, g):
    xf = x.astype(jnp.float32)
    y = xf * lax.rsqrt(jnp.mean(xf * xf, axis=-1, keepdims=True) + NORM_EPS)
    return (y * g.astype(jnp.float32)).astype(x.dtype)


def dwconv_centred(x, w, b=None):
    K = w.shape[0]
    pad = K // 2
    S = x.shape[1]
    xp = jnp.pad(x, ((0, 0), (pad, pad), (0, 0)))
    y = xp[:, 0:S] * w[0]
    for k in range(1, K):
        y = y + xp[:, k:k + S] * w[k]
    if b is not None:
        y = y + b
    return y


def short_conv_mixer(h, w_in, conv_w, w_out):
    b, c, v = jnp.split(h @ w_in, 3, axis=-1)
    return (b * dwconv_centred(c * v, conv_w)) @ w_out


def ssd_scan(x, dt, A, Bm, Cm):
    Bt, S = x.shape[0], x.shape[1]
    nc = S // CHUNK

    def chunks(t):
        return jnp.moveaxis(t.reshape((Bt, nc, CHUNK) + t.shape[2:]), 1, 0)

    mask = jnp.tril(jnp.ones((CHUNK, CHUNK), dtype=bool))[None, :, :, None, None]

    def step(state, inp):
        xc, dtc, Bc, Cc = inp
        acs = jnp.cumsum(dtc * A, axis=1)
        seg = acs[:, :, None] - acs[:, None, :]
        decay = jnp.exp(jnp.where(mask, seg, -jnp.inf))
        cb = jnp.einsum('blgn,bsgn->blsg', Cc, Bc)
        att = cb[..., None] * decay * dtc[:, None]
        y = jnp.einsum('blsgh,bsghp->blghp', att, xc)
        y = y + jnp.einsum('blgn,bghpn,blgh->blghp', Cc, state, jnp.exp(acs))
        w_end = jnp.exp(acs[:, -1:] - acs) * dtc
        state = state * jnp.exp(acs[:, -1])[..., None, None] + jnp.einsum('bsgn,bsgh,bsghp->bghpn', Bc, w_end, xc)
        return state, y

    state0 = jnp.zeros((Bt, SSM_GROUPS, HEADS_PER_GROUP, SSM_HEAD_DIM, D_STATE), jnp.float32)
    _, ys = lax.scan(step, state0, (chunks(x), chunks(dt), chunks(Bm), chunks(Cm)))
    return jnp.moveaxis(ys, 0, 1).reshape(x.shape)


def gated_group_rmsnorm(y, z, g):
    Bt, S, _ = y.shape
    u = (y.astype(jnp.float32) * jax.nn.silu(z.astype(jnp.float32))).reshape(Bt, S, SSM_GROUPS, D_INNER // SSM_GROUPS)
    u = u * lax.rsqrt(jnp.mean(u * u, axis=-1, keepdims=True) + SSM_NORM_EPS)
    return (u.reshape(Bt, S, D_INNER) * g.astype(jnp.float32)).astype(y.dtype)


def ssd_mixer(h, w_in, conv_w, conv_b, A_log, dt_bias, D_skip, norm_g, w_out):
    Bt, S, _ = h.shape
    z, xbc, dt = jnp.split(h @ w_in, [D_INNER, D_INNER + CONV_DIM], axis=-1)
    xbc = jax.nn.silu(dwconv_centred(xbc, conv_w, conv_b))
    xs, Bm, Cm = jnp.split(xbc, [D_INNER, D_INNER + SSM_GROUPS * D_STATE], axis=-1)
    xs = xs.astype(jnp.float32).reshape(Bt, S, SSM_GROUPS, HEADS_PER_GROUP, SSM_HEAD_DIM)
    Bm = Bm.astype(jnp.float32).reshape(Bt, S, SSM_GROUPS, D_STATE)
    Cm = Cm.astype(jnp.float32).reshape(Bt, S, SSM_GROUPS, D_STATE)
    dt = jax.nn.softplus(dt.astype(jnp.float32).reshape(Bt, S, 2, SSM_HEADS) + dt_bias.astype(jnp.float32))
    A = -jnp.exp(A_log.astype(jnp.float32)).reshape(2, SSM_GROUPS, HEADS_PER_GROUP)
    dt_f = dt[:, :, 0].reshape(Bt, S, SSM_GROUPS, HEADS_PER_GROUP)
    dt_b = dt[:, :, 1].reshape(Bt, S, SSM_GROUPS, HEADS_PER_GROUP)
    flip = lambda t: jnp.flip(t, axis=1)
    y_f = ssd_scan(xs, dt_f, A[0], Bm, Cm)
    y_b = flip(ssd_scan(flip(xs), flip(dt_b), A[1], flip(Bm), flip(Cm)))
    y = y_f + y_b + D_skip.astype(jnp.float32).reshape(SSM_GROUPS, HEADS_PER_GROUP)[..., None] * xs
    y = y.reshape(Bt, S, D_INNER).astype(h.dtype)
    return gated_group_rmsnorm(y, z, norm_g) @ w_out


def memory_cross_attention(h, mem_n, wq, wk, wv, wo):
    Bt, S, _ = h.shape
    M = mem_n.shape[1]
    q = (h @ wq).reshape(Bt, S, XA_HEADS, XA_HEAD_DIM)
    k = (mem_n @ wk).reshape(Bt, M, XA_HEADS, XA_HEAD_DIM)
    v = (mem_n @ wv).reshape(Bt, M, XA_HEADS, XA_HEAD_DIM)
    s = jnp.einsum('bqhd,bkhd->bhqk', q, k).astype(jnp.float32) * (XA_HEAD_DIM ** -0.5)
    p = jax.nn.softmax(s, axis=-1).astype(v.dtype)
    o = jnp.einsum('bhqk,bkhd->bqhd', p, v).reshape(Bt, S, XA_WIDTH)
    return o @ wo


def conv_gated_mlp(h, w_up, conv_w, conv_b, w_down):
    g, v = jnp.split(h @ w_up, 2, axis=-1)
    g = dwconv_centred(g, conv_w, conv_b)
    return (jax.nn.silu(g) * v) @ w_down


def trunk(x, mem, g_mix, g_xattn, g_mem, g_ffn, g_final,
          sc_w_in, sc_conv_w, sc_w_out,
          ssm_w_in, ssm_conv_w, ssm_conv_b, ssm_A_log, ssm_dt_bias, ssm_D, ssm_norm_g, ssm_w_out,
          xa_wq, xa_wk, xa_wv, xa_wo,
          ffn_w_up, ffn_conv_w, ffn_conv_b, ffn_w_down):
    for i in range(DEPTH):
        j = i // N_MIXERS
        h = rmsnorm(x, g_mix[i])
        if i % N_MIXERS == 0:
            x = x + short_conv_mixer(h, sc_w_in[j], sc_conv_w[j], sc_w_out[j])
        else:
            x = x + ssd_mixer(h, ssm_w_in[j], ssm_conv_w[j], ssm_conv_b[j], ssm_A_log[j],
                              ssm_dt_bias[j], ssm_D[j], ssm_norm_g[j], ssm_w_out[j])
        x = x + memory_cross_attention(rmsnorm(x, g_xattn[i]), rmsnorm(mem, g_mem[i]),
                                       xa_wq[i], xa_wk[i], xa_wv[i], xa_wo[i])
        x = x + conv_gated_mlp(rmsnorm(x, g_ffn[i]), ffn_w_up[i], ffn_conv_w[i], ffn_conv_b[i], ffn_w_down[i])
    return rmsnorm(x, g_final)


def setup_inputs(seed: int = 0) -> dict:
    key = jax.random.key(seed)
    ks = iter(jax.random.split(key, 40))
    f32 = jnp.float32

    def nrm(shape, fan_in):
        return jax.random.normal(next(ks), shape, f32) * (fan_in ** -0.5)

    def gain(shape):
        return 1.0 + 0.02 * jax.random.normal(next(ks), shape, f32)

    def small(shape):
        return 0.01 * jax.random.normal(next(ks), shape, f32)

    x_prompt = jax.random.normal(next(ks), (BATCH, SEQ, D_MODEL), f32)
    x_sample = jax.random.normal(next(ks), (DEC_BATCH, DEC_SEQ, D_MODEL), f32)
    mem_prompt = jax.random.normal(next(ks), (BATCH, N_MEM, D_MODEL), f32)
    mem_sample = jax.random.normal(next(ks), (DEC_BATCH, N_MEM, D_MODEL), f32)

    dt0 = jnp.exp(jax.random.uniform(next(ks), (N_B, 2, SSM_HEADS), f32)
                  * (math.log(0.1) - math.log(0.001)) + math.log(0.001))
    dt0 = jnp.maximum(dt0, 1e-4)
    ssm_dt_bias = dt0 + jnp.log(-jnp.expm1(-dt0))
    ssm_A_log = jnp.log(jax.random.uniform(next(ks), (N_B, 2, SSM_HEADS), f32, 1.0, 16.0))

    return {
        "x_prompt": x_prompt,
        "x_sample": x_sample,
        "mem_prompt": mem_prompt,
        "mem_sample": mem_sample,
        "g_mix": gain((DEPTH, D_MODEL)),
        "g_xattn": gain((DEPTH, D_MODEL)),
        "g_mem": gain((DEPTH, D_MODEL)),
        "g_ffn": gain((DEPTH, D_MODEL)),
        "g_final": gain((D_MODEL,)),
        "sc_w_in": nrm((N_A, D_MODEL, 3 * D_MODEL), D_MODEL),
        "sc_conv_w": nrm((N_A, SC_WIDTH, D_MODEL), SC_WIDTH),
        "sc_w_out": nrm((N_A, D_MODEL, D_MODEL), D_MODEL),
        "ssm_w_in": nrm((N_B, D_MODEL, SSM_IN_DIM), D_MODEL),
        "ssm_conv_w": nrm((N_B, SSM_CONV_WIDTH, CONV_DIM), SSM_CONV_WIDTH),
        "ssm_conv_b": small((N_B, CONV_DIM)),
        "ssm_A_log": ssm_A_log,
        "ssm_dt_bias": ssm_dt_bias,
        "ssm_D": gain((N_B, SSM_HEADS)),
        "ssm_norm_g": gain((N_B, D_INNER)),
        "ssm_w_out": nrm((N_B, D_INNER, D_MODEL), D_INNER),
        "xa_wq": nrm((DEPTH, D_MODEL, XA_WIDTH), D_MODEL),
        "xa_wk": nrm((DEPTH, D_MODEL, XA_WIDTH), D_MODEL),
        "xa_wv": nrm((DEPTH, D_MODEL, XA_WIDTH), D_MODEL),
        "xa_wo": nrm((DEPTH, XA_WIDTH, D_MODEL), XA_WIDTH),
        "ffn_w_up": nrm((DEPTH, D_MODEL, 2 * D_FF), D_MODEL),
        "ffn_conv_w": nrm((DEPTH, FFN_CONV_WIDTH, D_FF), FFN_CONV_WIDTH),
        "ffn_conv_b": small((DEPTH, D_FF)),
        "ffn_w_down": nrm((DEPTH, D_FF, D_MODEL), D_FF),
    }


def reference(x_prompt, x_sample, mem_prompt, mem_sample,
              g_mix, g_xattn, g_mem, g_ffn, g_final,
              sc_w_in, sc_conv_w, sc_w_out,
              ssm_w_in, ssm_conv_w, ssm_conv_b, ssm_A_log, ssm_dt_bias, ssm_D, ssm_norm_g, ssm_w_out,
              xa_wq, xa_wk, xa_wv, xa_wo,
              ffn_w_up, ffn_conv_w, ffn_conv_b, ffn_w_down):
    y_prompt = trunk(x_prompt, mem_prompt, g_mix, g_xattn, g_mem, g_ffn, g_final,
                     sc_w_in, sc_conv_w, sc_w_out,
                     ssm_w_in, ssm_conv_w, ssm_conv_b, ssm_A_log, ssm_dt_bias, ssm_D, ssm_norm_g, ssm_w_out,
                     xa_wq, xa_wk, xa_wv, xa_wo,
                     ffn_w_up, ffn_conv_w, ffn_conv_b, ffn_w_down)
    y_sample = trunk(x_sample, mem_sample, g_mix, g_xattn, g_mem, g_ffn, g_final,
                     sc_w_in, sc_conv_w, sc_w_out,
                     ssm_w_in, ssm_conv_w, ssm_conv_b, ssm_A_log, ssm_dt_bias, ssm_D, ssm_norm_g, ssm_w_out,
                     xa_wq, xa_wk, xa_wv, xa_wo,
                     ffn_w_up, ffn_conv_w, ffn_conv_b, ffn_w_down)
    return (y_prompt, y_sample)
```

```python
import functools

import jax
import jax.numpy as jnp
from jax import lax
from jax.experimental import pallas as pl
from jax.experimental.pallas import tpu as pltpu

F32 = jnp.float32
BF16 = jnp.bfloat16

NORM_EPS = 1e-6
SSM_NORM_EPS = 1e-5
SSM_HEAD_DIM = 64
SSM_GROUPS = 8
D_STATE = 128
CHUNK = 128
XA_HEADS = 4
XA_HEAD_DIM = 128

LANES = 128
BF16_SUBLANES = 16
HALO = BF16_SUBLANES
V7X_VMEM_BYTES = 64 * 1024 * 1024
VMEM_LIMIT = V7X_VMEM_BYTES - 8 * 1024 * 1024


def _params(n_axes):
    return pltpu.CompilerParams(
        dimension_semantics=("arbitrary",) * n_axes,
        vmem_limit_bytes=VMEM_LIMIT)


def _pick(n, prefs):
    for p in prefs:
        if n % p == 0:
            return p
    return n


def _rmsnorm_kernel(x_ref, g_ref, o_ref):
    x = x_ref[...]
    ms = jnp.mean(x * x, axis=-1, keepdims=True)
    o_ref[...] = (x * lax.rsqrt(ms + NORM_EPS) * g_ref[...]).astype(o_ref.dtype)


def _rmsnorm(x, g):
    m, d = x.shape
    tm = _pick(m, (512, 256, 128))
    return pl.pallas_call(
        _rmsnorm_kernel,
        grid=(m // tm,),
        in_specs=[pl.BlockSpec((tm, d), lambda i: (i, 0)),
                  pl.BlockSpec((1, d), lambda i: (0, 0))],
        out_specs=pl.BlockSpec((tm, d), lambda i: (i, 0)),
        out_shape=jax.ShapeDtypeStruct((m, d), BF16),
        compiler_params=_params(1),
        name="rmsnorm",
    )(x, g.reshape(1, d))


def _matmul_kernel(a_ref, w_ref, o_ref):
    o_ref[...] = jnp.dot(a_ref[...], w_ref[...],
                         preferred_element_type=F32).astype(o_ref.dtype)


def _matmul(a, w, out_dtype=BF16):
    m, k = a.shape
    n = w.shape[1]
    tm = _pick(m, (1024, 512, 256, 128))
    tn = _pick(n, (1024, 512, 256, 128))
    return pl.pallas_call(
        _matmul_kernel,
        grid=(m // tm, n // tn),
        in_specs=[pl.BlockSpec((tm, k), lambda i, j: (i, 0)),
                  pl.BlockSpec((k, tn), lambda i, j: (0, j))],
        out_specs=pl.BlockSpec((tm, tn), lambda i, j: (i, j)),
        out_shape=jax.ShapeDtypeStruct((m, n), out_dtype),
        compiler_params=_params(2),
        name="matmul",
    )(a, w)


def _conv3_rows(u, u_prev, u_next, w):
    t = u.shape[0]
    rid = lax.broadcasted_iota(jnp.int32, u.shape, 0)
    below = jnp.where(rid == 0, u_prev, pltpu.roll(u, 1, axis=0))
    above = jnp.where(rid == t - 1, u_next, pltpu.roll(u, t - 1, axis=0))
    return below * w[0:1, :] + u * w[1:2, :] + above * w[2:3, :]


def _halo_flags(tm, seq):
    i = pl.program_id(0)
    has_prev = (i * tm) % seq != 0
    has_next = ((i + 1) * tm) % seq != 0
    return has_prev, has_next


def _halo_specs(tm, tc, col0, m, ncol_axis):
    per = tm // HALO
    last = m // HALO - 1
    if ncol_axis:
        prev = pl.BlockSpec((HALO, tc), lambda i, c: (jnp.maximum(i * per - 1, 0), col0 + c))
        nxt = pl.BlockSpec((HALO, tc), lambda i, c: (jnp.minimum((i + 1) * per, last), col0 + c))
    else:
        prev = pl.BlockSpec((HALO, tc), lambda i: (jnp.maximum(i * per - 1, 0), col0))
        nxt = pl.BlockSpec((HALO, tc), lambda i: (jnp.minimum((i + 1) * per, last), col0))
    return prev, nxt


def _plain_prologue(tm, seq, refs):
    return refs[0][...]


def _sc_prologue(tm, seq, refs):
    b_ref, c_ref, v_ref, cp_ref, cn_ref, vp_ref, vn_ref, w_ref = refs
    has_prev, has_next = _halo_flags(tm, seq)
    u = c_ref[...].astype(F32) * v_ref[...].astype(F32)
    u_prev = cp_ref[HALO - 1:HALO, :].astype(F32) * vp_ref[HALO - 1:HALO, :].astype(F32)
    u_next = cn_ref[0:1, :].astype(F32) * vn_ref[0:1, :].astype(F32)
    u_prev = jnp.where(has_prev, u_prev, 0.0)
    u_next = jnp.where(has_next, u_next, 0.0)
    y = _conv3_rows(u, u_prev, u_next, w_ref[...])
    return (b_ref[...].astype(F32) * y).astype(BF16)


def _ffn_prologue(tm, seq, refs):
    g_ref, v_ref, gp_ref, gn_ref, w_ref, b_ref = refs
    has_prev, has_next = _halo_flags(tm, seq)
    g = g_ref[...].astype(F32)
    g_prev = jnp.where(has_prev, gp_ref[HALO - 1:HALO, :].astype(F32), 0.0)
    g_next = jnp.where(has_next, gn_ref[0:1, :].astype(F32), 0.0)
    y = _conv3_rows(g, g_prev, g_next, w_ref[...]) + b_ref[...]
    return (y * jax.nn.sigmoid(y) * v_ref[...].astype(F32)).astype(BF16)


def _proj_kernel(prologue, n_in, nk, n_slabs, final, tm, seq, *refs):
    in_refs = refs[:n_in]
    w_ref, x_ref, g_ref = refs[n_in:n_in + 3]
    out_refs = refs[n_in + 3:]
    acc_ref = out_refs[0]
    k = pl.program_id(1)
    d = acc_ref.shape[1]
    slab = d // n_slabs

    a = prologue(tm, seq, in_refs)
    part = jnp.dot(a, w_ref[...], preferred_element_type=F32)

    @pl.when(k == 0)
    def _():
        acc_ref[...] = part

    @pl.when(k > 0)
    def _():
        acc_ref[...] += part

    for s in range(n_slabs):
        @pl.when(k == s)
        def _(s=s):
            acc_ref[:, s * slab:(s + 1) * slab] += x_ref[...]

    @pl.when(k == nk - 1)
    def _():
        xn = acc_ref[...]
        ms = jnp.mean(xn * xn, axis=-1, keepdims=True)
        hn = xn * lax.rsqrt(ms + NORM_EPS) * g_ref[...]
        if final:
            acc_ref[...] = hn
        else:
            out_refs[1][...] = hn.astype(BF16)


def _proj(prologue, ins, in_specs, w, x, g, *, seq, tm, tk, final):
    m, d = x.shape
    kdim = w.shape[0]
    nk = kdim // tk
    n_slabs = max(s for s in range(1, min(nk, 8) + 1) if d % (s * LANES) == 0)
    slab = d // n_slabs
    kern = functools.partial(_proj_kernel, prologue, len(ins), nk, n_slabs, final, tm, seq)
    specs = list(in_specs) + [
        pl.BlockSpec((tk, d), lambda i, k: (k, 0)),
        pl.BlockSpec((tm, slab), lambda i, k: (i, jnp.minimum(k, n_slabs - 1))),
        pl.BlockSpec((1, d), lambda i, k: (0, 0)),
    ]
    x_spec = pl.BlockSpec((tm, d), lambda i, k: (i, 0))
    if final:
        out_specs = x_spec
        out_shape = jax.ShapeDtypeStruct((m, d), F32)
    else:
        out_specs = (x_spec, pl.BlockSpec((tm, d), lambda i, k: (i, 0)))
        out_shape = (jax.ShapeDtypeStruct((m, d), F32), jax.ShapeDtypeStruct((m, d), BF16))
    return pl.pallas_call(
        kern,
        grid=(m // tm, nk),
        in_specs=specs,
        out_specs=out_specs,
        out_shape=out_shape,
        compiler_params=_params(2),
        name="out_proj",
    )(*ins, w, x, g.reshape(1, d))


def _proj_tiles(seq, kdim):
    return _pick(seq, (512, 256, 128)), _pick(kdim, (512, 256, 128))


def _proj_plain(a, w, x, g, *, seq):
    tm, tk = _proj_tiles(seq, w.shape[0])
    specs = [pl.BlockSpec((tm, tk), lambda i, k: (i, k))]
    return _proj(_plain_prologue, [a], specs, w, x, g, seq=seq, tm=tm, tk=tk, final=False)


def _proj_sc(bcv, conv_w, w, x, g, *, seq):
    m, d = x.shape
    tm, tk = _proj_tiles(seq, d)
    nb = d // tk
    main = lambda off: pl.BlockSpec((tm, tk), lambda i, k: (i, off + k))
    cp, cn = _halo_specs(tm, tk, nb, m, True)
    vp, vn = _halo_specs(tm, tk, 2 * nb, m, True)
    specs = [main(0), main(nb), main(2 * nb), cp, cn, vp, vn,
             pl.BlockSpec((3, tk), lambda i, k: (0, k))]
    ins = [bcv, bcv, bcv, bcv, bcv, bcv, bcv, conv_w]
    return _proj(_sc_prologue, ins, specs, w, x, g, seq=seq, tm=tm, tk=tk, final=False)


def _proj_ffn(gv, conv_w, conv_b, w, x, g, *, seq, final):
    m, d = x.shape
    dff = w.shape[0]
    tm, tk = _proj_tiles(seq, dff)
    nb = dff // tk
    main = lambda off: pl.BlockSpec((tm, tk), lambda i, k: (i, off + k))
    gp, gn = _halo_specs(tm, tk, 0, m, True)
    specs = [main(0), main(nb), gp, gn,
             pl.BlockSpec((3, tk), lambda i, k: (0, k)),
             pl.BlockSpec((1, tk), lambda i, k: (0, k))]
    ins = [gv, gv, gv, gv, conv_w, conv_b.reshape(1, dff)]
    return _proj(_ffn_prologue, ins, specs, w, x, g, seq=seq, tm=tm, tk=tk, final=final)


def _conv_silu_kernel(tm, seq, x_ref, xp_ref, xn_ref, w_ref, b_ref, o_ref):
    has_prev, has_next = _halo_flags(tm, seq)
    x = x_ref[...].astype(F32)
    x_prev = jnp.where(has_prev, xp_ref[HALO - 1:HALO, :].astype(F32), 0.0)
    x_next = jnp.where(has_next, xn_ref[0:1, :].astype(F32), 0.0)
    y = _conv3_rows(x, x_prev, x_next, w_ref[...]) + b_ref[...]
    o_ref[...] = (y * jax.nn.sigmoid(y)).astype(o_ref.dtype)


def _conv_silu(zx, col_start, conv_w, conv_b, *, seq):
    m = zx.shape[0]
    ncols = conv_w.shape[1]
    tm = _pick(seq, (512, 256, 128))
    tc = _pick(ncols, (2048, 1024, 512, 256, 128))
    assert col_start % tc == 0
    col0 = col_start // tc
    xp, xn = _halo_specs(tm, tc, col0, m, True)
    return pl.pallas_call(
        functools.partial(_conv_silu_kernel, tm, seq),
        grid=(m // tm, ncols // tc),
        in_specs=[pl.BlockSpec((tm, tc), lambda i, c: (i, col0 + c)), xp, xn,
                  pl.BlockSpec((3, tc), lambda i, c: (0, c)),
                  pl.BlockSpec((1, tc), lambda i, c: (0, c))],
        out_specs=pl.BlockSpec((tm, tc), lambda i, c: (i, c)),
        out_shape=jax.ShapeDtypeStruct((m, ncols), BF16),
        compiler_params=_params(2),
        name="ssd_conv_silu",
    )(zx, zx, zx, conv_w, conv_b.reshape(1, ncols))


def _cumsum_rows(a, reverse):
    n = a.shape[0]
    rid = lax.broadcasted_iota(jnp.int32, a.shape, 0)
    acc = a
    d = 1
    while d < n:
        if reverse:
            acc = acc + jnp.where(rid < n - d, pltpu.roll(acc, n - d, axis=0), 0.0)
        else:
            acc = acc + jnp.where(rid >= d, pltpu.roll(acc, d, axis=0), 0.0)
        d *= 2
    return acc


def _scan_kernel(reverse, n_heads, *refs):
    if reverse:
        (x_ref, bc_ref, dt_ref, dtb_ref, alog_ref, yf_ref, z_ref, ng_ref, o_ref, st_ref) = refs
    else:
        (x_ref, bc_ref, dt_ref, dtb_ref, alog_ref, dskip_ref, o_ref, st_ref) = refs
    L = CHUNK
    hg = n_heads // SSM_GROUPS
    gw = hg * SSM_HEAD_DIM
    pairs = hg // 2
    pw = 2 * SSM_HEAD_DIM

    @pl.when(pl.program_id(1) == 0)
    def _():
        st_ref[...] = jnp.zeros_like(st_ref)

    dt_raw = dt_ref[...] + dtb_ref[...]
    dt = jnp.maximum(dt_raw, 0.0) + jnp.log1p(jnp.exp(-jnp.abs(dt_raw)))
    a = dt * (-jnp.exp(alog_ref[...]))
    acs = _cumsum_rows(a, reverse)
    edge = acs[0:1, :] if reverse else acs[L - 1:L, :]
    eacs = jnp.exp(acs)
    wend = jnp.exp(edge - acs) * dt
    dtot = jnp.exp(edge)
    acs_t = acs.T
    dt_t = dt.T
    wend_t = wend.T

    row = lax.broadcasted_iota(jnp.int32, (L, L), 0)
    col = lax.broadcasted_iota(jnp.int32, (L, L), 1)
    keep = (row <= col) if reverse else (row >= col)
    lane = lax.broadcasted_iota(jnp.int32, (L, pw), 1)
    first = lane < SSM_HEAD_DIM
    lane1 = lax.broadcasted_iota(jnp.int32, (1, pw), 1)
    first1 = lane1 < SSM_HEAD_DIM

    for g in range(SSM_GROUPS):
        b_g = bc_ref[:, g * D_STATE:(g + 1) * D_STATE]
        c_g = bc_ref[:, (SSM_GROUPS + g) * D_STATE:(SSM_GROUPS + g + 1) * D_STATE]
        cb = lax.dot_general(c_g, b_g, (((1,), (1,)), ((), ())),
                             preferred_element_type=F32)
        b_t = b_g.astype(F32).T
        z_all = jnp.dot(c_g, st_ref[g].astype(BF16), preferred_element_type=F32)
        ys = []
        for j in range(pairs):
            h0 = g * hg + 2 * j
            h1 = h0 + 1
            c0 = g * gw + j * pw
            xp = x_ref[:, c0:c0 + pw]
            zero = jnp.zeros_like(xp)
            rhs = jnp.concatenate([jnp.where(first, xp, zero),
                                   jnp.where(first, zero, xp)], axis=0)
            atts, bws = [], []
            for h in (h0, h1):
                seg = acs[:, h:h + 1] - acs_t[h:h + 1, :]
                dec = jnp.exp(jnp.where(keep, seg, -jnp.inf))
                atts.append((cb * dec * dt_t[h:h + 1, :]).astype(BF16))
                bws.append((b_t * wend_t[h:h + 1, :]).astype(BF16))
            y = jnp.dot(jnp.concatenate(atts, axis=1), rhs, preferred_element_type=F32)
            esc = jnp.where(first, eacs[:, h0:h0 + 1], eacs[:, h1:h1 + 1])
            y = y + z_all[:, j * pw:(j + 1) * pw] * esc
            s_old = st_ref[g, :, j * pw:(j + 1) * pw]
            s_dec = jnp.where(first1, dtot[:, h0:h0 + 1], dtot[:, h1:h1 + 1])
            st_ref[g, :, j * pw:(j + 1) * pw] = s_old * s_dec + jnp.dot(
                jnp.concatenate(bws, axis=1), rhs, preferred_element_type=F32)
            if reverse:
                ys.append(y)
            else:
                o_ref[:, c0:c0 + pw] = (y + dskip_ref[:, c0:c0 + pw] * xp.astype(F32)
                                        ).astype(o_ref.dtype)
        if reverse:
            sl = slice(g * gw, (g + 1) * gw)
            zz = z_ref[:, sl].astype(F32)
            u = (jnp.concatenate(ys, axis=1) + yf_ref[:, sl].astype(F32)) * (zz * jax.nn.sigmoid(zz))
            ms = jnp.mean(u * u, axis=-1, keepdims=True)
            o_ref[:, sl] = (u * lax.rsqrt(ms + SSM_NORM_EPS) * ng_ref[:, sl]).astype(o_ref.dtype)


def _ssd_scan(reverse, xbc, dt_raw, dt_bias, a_log, extra, *, batch, seq):
    m = xbc.shape[0]
    n_heads = dt_bias.shape[1]
    d_inner = n_heads * SSM_HEAD_DIM
    bc_w = 2 * SSM_GROUPS * D_STATE
    assert d_inner % bc_w == 0
    nc = seq // CHUNK
    L = CHUNK
    d = 1 if reverse else 0

    def rows(b, c):
        return b * nc + (nc - 1 - c if reverse else c)

    specs = [
        pl.BlockSpec((L, d_inner), lambda b, c: (rows(b, c), 0)),
        pl.BlockSpec((L, bc_w), lambda b, c: (rows(b, c), d_inner // bc_w)),
        pl.BlockSpec((L, n_heads), lambda b, c: (rows(b, c), d)),
        pl.BlockSpec((1, n_heads), lambda b, c: (0, 0)),
        pl.BlockSpec((1, n_heads), lambda b, c: (0, 0)),
    ]
    ins = [xbc, xbc, dt_raw, dt_bias[d:d + 1], a_log[d:d + 1]]
    if reverse:
        yf, zx, norm_g = extra
        specs += [pl.BlockSpec((L, d_inner), lambda b, c: (rows(b, c), 0)),
                  pl.BlockSpec((L, d_inner), lambda b, c: (rows(b, c), 0)),
                  pl.BlockSpec((1, d_inner), lambda b, c: (0, 0))]
        ins += [yf, zx, norm_g.reshape(1, d_inner)]
    else:
        (d_skip,) = extra
        specs += [pl.BlockSpec((1, d_inner), lambda b, c: (0, 0))]
        ins += [jnp.repeat(d_skip, SSM_HEAD_DIM).reshape(1, d_inner)]
    out_dtype = BF16
    gw = d_inner // SSM_GROUPS
    return pl.pallas_call(
        functools.partial(_scan_kernel, reverse, n_heads),
        grid=(batch, nc),
        in_specs=specs,
        out_specs=pl.BlockSpec((L, d_inner), lambda b, c: (rows(b, c), 0)),
        out_shape=jax.ShapeDtypeStruct((m, d_inner), out_dtype),
        scratch_shapes=[pltpu.VMEM((SSM_GROUPS, D_STATE, gw), F32)],
        compiler_params=_params(2),
        name="ssd_scan_bwd" if reverse else "ssd_scan_fwd",
    )(*ins)


def _xattn_kernel(h_ref, x_ref, wq_ref, kv_ref, wo_ref, g_ref, xo_ref, ho_ref):
    width = XA_HEADS * XA_HEAD_DIM
    q = jnp.dot(h_ref[...], wq_ref[...], preferred_element_type=F32).astype(BF16)
    outs = []
    for hd in range(XA_HEADS):
        sl = slice(hd * XA_HEAD_DIM, (hd + 1) * XA_HEAD_DIM)
        k_h = kv_ref[:, sl]
        v_h = kv_ref[:, width + hd * XA_HEAD_DIM: width + (hd + 1) * XA_HEAD_DIM]
        s = lax.dot_general(q[:, sl], k_h, (((1,), (1,)), ((), ())),
                            preferred_element_type=F32) * (XA_HEAD_DIM ** -0.5)
        p = jnp.exp(s - jnp.max(s, axis=-1, keepdims=True))
        denom = jnp.sum(p, axis=-1, keepdims=True)
        o_h = jnp.dot(p.astype(BF16), v_h, preferred_element_type=F32)
        outs.append(o_h / denom)
    o = jnp.concatenate(outs, axis=-1).astype(BF16)
    xn = x_ref[...] + jnp.dot(o, wo_ref[...], preferred_element_type=F32)
    xo_ref[...] = xn
    ms = jnp.mean(xn * xn, axis=-1, keepdims=True)
    ho_ref[...] = (xn * lax.rsqrt(ms + NORM_EPS) * g_ref[...]).astype(BF16)


def _xattn(h, x, wq, kv, wo, g, *, seq):
    m, d = x.shape
    n_mem, kvw = kv.shape[1], kv.shape[2]
    width = XA_HEADS * XA_HEAD_DIM
    tm = _pick(seq, (256, 128))
    row = pl.BlockSpec((tm, d), lambda i: (i, 0))
    return pl.pallas_call(
        _xattn_kernel,
        grid=(m // tm,),
        in_specs=[row, row,
                  pl.BlockSpec((d, width), lambda i: (0, 0)),
                  pl.BlockSpec((None, n_mem, kvw), lambda i: ((i * tm) // seq, 0, 0)),
                  pl.BlockSpec((width, d), lambda i: (0, 0)),
                  pl.BlockSpec((1, d), lambda i: (0, 0))],
        out_specs=(row, row),
        out_shape=(jax.ShapeDtypeStruct((m, d), F32), jax.ShapeDtypeStruct((m, d), BF16)),
        compiler_params=_params(1),
        name="xattn",
    )(h, x, wq, kv, wo, g.reshape(1, d))


def _trunk(x, mem, p):
    batch, seq, d = x.shape
    m = batch * seq
    n_mem = mem.shape[1]
    depth = p["g_mix"].shape[0]
    d_inner = p["ssm_norm_g"].shape[1]
    xf = x.reshape(m, d)
    memf = mem.reshape(batch * n_mem, d)
    h = _rmsnorm(xf, p["g_mix"][0])
    out = None
    for i in range(depth):
        j = i // 2
        if i % 2 == 0:
            bcv = _matmul(h, p["sc_w_in"][j])
            xf, h = _proj_sc(bcv, p["sc_conv_w"][j], p["sc_w_out"][j], xf, p["g_xattn"][i], seq=seq)
        else:
            zx = _matmul(h, p["ssm_w_zx"][j])
            dt_raw = _matmul(h, p["ssm_w_dt"][j], out_dtype=F32)
            xbc = _conv_silu(zx, d_inner, p["ssm_conv_w"][j], p["ssm_conv_b"][j], seq=seq)
            yf = _ssd_scan(False, xbc, dt_raw, p["ssm_dt_bias"][j], p["ssm_A_log"][j],
                           (p["ssm_D"][j],), batch=batch, seq=seq)
            a = _ssd_scan(True, xbc, dt_raw, p["ssm_dt_bias"][j], p["ssm_A_log"][j],
                          (yf, zx, p["ssm_norm_g"][j]), batch=batch, seq=seq)
            xf, h = _proj_plain(a, p["ssm_w_out"][j], xf, p["g_xattn"][i], seq=seq)
        kv = _matmul(_rmsnorm(memf, p["g_mem"][i]), p["xa_wkv"][i]).reshape(batch, n_mem, -1)
        xf, h = _xattn(h, xf, p["xa_wq"][i], kv, p["xa_wo"][i], p["g_ffn"][i], seq=seq)
        gv = _matmul(h, p["ffn_w_up"][i])
        if i + 1 < depth:
            xf, h = _proj_ffn(gv, p["ffn_conv_w"][i], p["ffn_conv_b"][i], p["ffn_w_down"][i],
                              xf, p["g_mix"][i + 1], seq=seq, final=False)
        else:
            out = _proj_ffn(gv, p["ffn_conv_w"][i], p["ffn_conv_b"][i], p["ffn_w_down"][i],
                            xf, p["g_final"], seq=seq, final=True)
    return out.reshape(batch, seq, d)


def _pad_to(a, axis, size):
    pad = [(0, 0)] * a.ndim
    pad[axis] = (0, size - a.shape[axis])
    return jnp.pad(a, pad)


def kernel(x_prompt, x_sample, mem_prompt, mem_sample, g_mix, g_xattn, g_mem, g_ffn, g_final, sc_w_in, sc_conv_w, sc_w_out, ssm_w_in, ssm_conv_w, ssm_conv_b, ssm_A_log, ssm_dt_bias, ssm_D, ssm_norm_g, ssm_w_out, xa_wq, xa_wk, xa_wv, xa_wo, ffn_w_up, ffn_conv_w, ffn_conv_b, ffn_w_down):
    d_inner = ssm_norm_g.shape[1]
    conv_dim = ssm_conv_w.shape[2]
    d_ff = ffn_w_down.shape[1]
    d_ff_pad = -(-d_ff // 512) * 512
    w_up = ffn_w_up.astype(BF16)
    p = {
        "g_mix": g_mix, "g_xattn": g_xattn, "g_mem": g_mem, "g_ffn": g_ffn, "g_final": g_final,
        "sc_w_in": sc_w_in.astype(BF16), "sc_conv_w": sc_conv_w, "sc_w_out": sc_w_out.astype(BF16),
        "ssm_w_zx": ssm_w_in[:, :, :d_inner + conv_dim].astype(BF16),
        "ssm_w_dt": ssm_w_in[:, :, d_inner + conv_dim:].astype(BF16),
        "ssm_conv_w": ssm_conv_w, "ssm_conv_b": ssm_conv_b,
        "ssm_A_log": ssm_A_log, "ssm_dt_bias": ssm_dt_bias, "ssm_D": ssm_D,
        "ssm_norm_g": ssm_norm_g, "ssm_w_out": ssm_w_out.astype(BF16),
        "xa_wq": xa_wq.astype(BF16),
        "xa_wkv": jnp.concatenate([xa_wk, xa_wv], axis=-1).astype(BF16),
        "xa_wo": xa_wo.astype(BF16),
        "ffn_w_up": jnp.concatenate([_pad_to(w_up[:, :, :d_ff], 2, d_ff_pad),
                                     _pad_to(w_up[:, :, d_ff:], 2, d_ff_pad)], axis=-1),
        "ffn_conv_w": _pad_to(ffn_conv_w, 2, d_ff_pad),
        "ffn_conv_b": _pad_to(ffn_conv_b, 1, d_ff_pad),
        "ffn_w_down": _pad_to(ffn_w_down.astype(BF16), 1, d_ff_pad),
    }
    return (_trunk(x_prompt, mem_prompt, p), _trunk(x_sample, mem_sample, p))
```

```python
import functools

import jax
import jax.numpy as jnp
from jax import lax
from jax.experimental import pallas as pl
from jax.experimental.pallas import tpu as pltpu

F32 = jnp.float32
BF16 = jnp.bfloat16

NORM_EPS = 1e-6
SSM_NORM_EPS = 1e-5
SSM_HEAD_DIM = 64
SSM_GROUPS = 8
D_STATE = 128
CHUNK = 128
XA_HEADS = 4
XA_HEAD_DIM = 128
LOG2_E = 1.4426950408889634

LANES = 128
F32_SUBLANES = 8
BF16_SUBLANES = 16
PROJ_N_CHUNK = 512
MATMUL_N_CHUNK = 1024
HALO = BF16_SUBLANES
V7X_VMEM_BYTES = 64 * 1024 * 1024
VMEM_LIMIT = V7X_VMEM_BYTES - 4 * 1024 * 1024


def _params(n_axes):
    return pltpu.CompilerParams(
        dimension_semantics=("arbitrary",) * n_axes,
        vmem_limit_bytes=VMEM_LIMIT)


def _pick(n, prefs):
    for p in prefs:
        if n % p == 0:
            return p
    return n


def _rmsnorm_kernel(x_ref, g_ref, o_ref):
    x = x_ref[...]
    ms = jnp.mean(x * x, axis=-1, keepdims=True)
    o_ref[...] = (x * lax.rsqrt(ms + NORM_EPS) * g_ref[...]).astype(o_ref.dtype)


def _rmsnorm(x, g):
    m, d = x.shape
    tm = _pick(m, (512, 256, 128))
    return pl.pallas_call(
        _rmsnorm_kernel,
        grid=(m // tm,),
        in_specs=[pl.BlockSpec((tm, d), lambda i: (i, 0)),
                  pl.BlockSpec((1, d), lambda i: (0, 0))],
        out_specs=pl.BlockSpec((tm, d), lambda i: (i, 0)),
        out_shape=jax.ShapeDtypeStruct((m, d), BF16),
        compiler_params=_params(1),
        name="rmsnorm",
    )(x, g.reshape(1, d))


def _matmul_kernel(a_ref, w_ref, o_ref):
    nchunk = _pick(o_ref.shape[1], (MATMUL_N_CHUNK, LANES))
    for n0 in range(0, o_ref.shape[1], nchunk):
        o_ref[:, n0:n0 + nchunk] = jnp.dot(a_ref[...], w_ref[:, n0:n0 + nchunk],
                                           preferred_element_type=F32).astype(o_ref.dtype)


def _matmul(a, w, out_dtype=BF16):
    m, k = a.shape
    n = w.shape[1]
    tm = _pick(m, (1024, 512, 256, 128))
    tn = _pick(n, (2048, 1024, 512, 256, 128))
    return pl.pallas_call(
        _matmul_kernel,
        grid=(m // tm, n // tn),
        in_specs=[pl.BlockSpec((tm, k), lambda i, j: (i, 0), pipeline_mode=pl.Buffered(1)),
                  pl.BlockSpec((k, tn), lambda i, j: (0, j))],
        out_specs=pl.BlockSpec((tm, tn), lambda i, j: (i, j)),
        out_shape=jax.ShapeDtypeStruct((m, n), out_dtype),
        compiler_params=_params(2),
        name="matmul",
    )(a, w)


def _conv3_rows(u, u_prev, u_next, w):
    t = u.shape[0]
    rid = lax.broadcasted_iota(jnp.int32, u.shape, 0)
    below = jnp.where(rid == 0, u_prev, pltpu.roll(u, 1, axis=0))
    above = jnp.where(rid == t - 1, u_next, pltpu.roll(u, t - 1, axis=0))
    return below * w[0:1, :] + u * w[1:2, :] + above * w[2:3, :]


def _halo_flags(tm, seq, i):
    has_prev = (i * tm) % seq != 0
    has_next = ((i + 1) * tm) % seq != 0
    return has_prev, has_next


def _halo_specs(tm, tc, col0, m, pos):
    per = tm // HALO
    last = m // HALO - 1

    def prev(*idx):
        i, c = pos(*idx)
        return jnp.maximum(i * per - 1, 0), col0 + c

    def nxt(*idx):
        i, c = pos(*idx)
        return jnp.minimum((i + 1) * per, last), col0 + c

    return pl.BlockSpec((HALO, tc), prev), pl.BlockSpec((HALO, tc), nxt)


def _sc_prologue(tm, seq, i, refs):
    b_ref, c_ref, v_ref, cp_ref, cn_ref, vp_ref, vn_ref, w_ref = refs
    has_prev, has_next = _halo_flags(tm, seq, i)
    u = c_ref[...].astype(F32) * v_ref[...].astype(F32)
    u_prev = cp_ref[HALO - 1:HALO, :].astype(F32) * vp_ref[HALO - 1:HALO, :].astype(F32)
    u_next = cn_ref[0:1, :].astype(F32) * vn_ref[0:1, :].astype(F32)
    u_prev = jnp.where(has_prev, u_prev, 0.0)
    u_next = jnp.where(has_next, u_next, 0.0)
    y = _conv3_rows(u, u_prev, u_next, w_ref[...])
    return (b_ref[...].astype(F32) * y).astype(BF16)


def _ffn_prologue(tm, seq, i, refs):
    g_ref, v_ref, gp_ref, gn_ref, w_ref, b_ref = refs
    has_prev, has_next = _halo_flags(tm, seq, i)
    g = g_ref[...].astype(F32)
    g_prev = jnp.where(has_prev, gp_ref[HALO - 1:HALO, :].astype(F32), 0.0)
    g_next = jnp.where(has_next, gn_ref[0:1, :].astype(F32), 0.0)
    y = _conv3_rows(g, g_prev, g_next, w_ref[...]) + b_ref[...]
    return (y * jax.nn.sigmoid(y) * v_ref[...].astype(F32)).astype(BF16)


def _proj_kernel(prologue, n_in, nk, n_slabs, final, tm, seq, *refs):
    in_refs = refs[:n_in]
    w_ref, x_ref, g_ref = refs[n_in:n_in + 3]
    out_refs = refs[n_in + 3:]
    acc_ref = out_refs[0]
    i = pl.program_id(0)
    k = pl.program_id(1)
    d = acc_ref.shape[1]
    nchunk = _pick(d, (PROJ_N_CHUNK, LANES))

    def step(first):
        a = in_refs[0][...] if prologue is None else prologue(tm, seq, i, in_refs)
        for n0 in range(0, d, nchunk):
            part = jnp.dot(a, w_ref[:, n0:n0 + nchunk], preferred_element_type=F32)
            if first:
                acc_ref[:, n0:n0 + nchunk] = part
            else:
                acc_ref[:, n0:n0 + nchunk] += part

    pl.when(k == 0)(functools.partial(step, True))
    pl.when(k > 0)(functools.partial(step, False))

    rows = tm // n_slabs

    @pl.when(k < n_slabs)
    def _():
        r0 = pl.multiple_of(k * rows, rows)
        acc_ref[pl.ds(r0, rows), :] += x_ref[...]

    @pl.when(k == nk - 1)
    def _():
        xn = acc_ref[...]
        ms = jnp.mean(xn * xn, axis=-1, keepdims=True)
        hn = xn * lax.rsqrt(ms + NORM_EPS) * g_ref[...]
        if final:
            acc_ref[...] = hn
        else:
            out_refs[1][...] = hn.astype(BF16)


def _proj(prologue, ins, in_specs, w, x, g, *, seq, tm, tk, final):
    m, d = x.shape
    nk = w.shape[0] // tk
    n_slabs = max(s for s in range(1, min(nk, 8) + 1) if tm % (s * F32_SUBLANES) == 0)
    rows = tm // n_slabs
    kern = functools.partial(_proj_kernel, prologue, len(ins), nk, n_slabs, final, tm, seq)
    specs = list(in_specs) + [
        pl.BlockSpec((tk, d), lambda i, k: (k, 0)),
        pl.BlockSpec((rows, d), lambda i, k: (i * n_slabs + jnp.minimum(k, n_slabs - 1), 0)),
        pl.BlockSpec((1, d), lambda i, k: (0, 0)),
    ]
    x_spec = pl.BlockSpec((tm, d), lambda i, k: (i, 0))
    if final:
        out_specs = x_spec
        out_shape = jax.ShapeDtypeStruct((m, d), F32)
    else:
        out_specs = (x_spec, pl.BlockSpec((tm, d), lambda i, k: (i, 0)))
        out_shape = (jax.ShapeDtypeStruct((m, d), F32), jax.ShapeDtypeStruct((m, d), BF16))
    return pl.pallas_call(
        kern,
        grid=(m // tm, nk),
        in_specs=specs,
        out_specs=out_specs,
        out_shape=out_shape,
        compiler_params=_params(2),
        name="out_proj",
    )(*ins, w, x, g.reshape(1, d))


def _proj_tiles(seq, kdim, tk_max=1024):
    tks = tuple(t for t in (1024, 512, 256, 128) if t <= tk_max)
    return _pick(seq, (512, 256, 128)), _pick(kdim, tks)


def _proj_plain(a, w, x, g, *, seq):
    tm, tk = _proj_tiles(seq, w.shape[0])
    specs = [pl.BlockSpec((tm, tk), lambda i, k: (i, k))]
    return _proj(None, [a], specs, w, x, g, seq=seq, tm=tm, tk=tk, final=False)


def _tile_pos(i, k):
    return i, k


def _proj_sc(bcv, conv_w, w, x, g, *, seq):
    m, d = x.shape
    tm, tk = _proj_tiles(seq, d, tk_max=512)
    nb = d // tk
    main = lambda off: pl.BlockSpec((tm, tk), lambda i, k: (i, off + k))
    cp, cn = _halo_specs(tm, tk, nb, m, _tile_pos)
    vp, vn = _halo_specs(tm, tk, 2 * nb, m, _tile_pos)
    specs = [main(0), main(nb), main(2 * nb), cp, cn, vp, vn,
             pl.BlockSpec((3, tk), lambda i, k: (0, k))]
    ins = [bcv, bcv, bcv, bcv, bcv, bcv, bcv, conv_w]
    return _proj(_sc_prologue, ins, specs, w, x, g, seq=seq, tm=tm, tk=tk, final=False)


def _proj_ffn(gv, conv_w, conv_b, w, x, g, *, seq, final):
    m, d = x.shape
    dff = w.shape[0]
    tm, tk = _proj_tiles(seq, dff)
    nb = dff // tk
    main = lambda off: pl.BlockSpec((tm, tk), lambda i, k: (i, off + k))
    gp, gn = _halo_specs(tm, tk, 0, m, _tile_pos)
    specs = [main(0), main(nb), gp, gn,
             pl.BlockSpec((3, tk), lambda i, k: (0, k)),
             pl.BlockSpec((1, tk), lambda i, k: (0, k))]
    ins = [gv, gv, gv, gv, conv_w, conv_b.reshape(1, dff)]
    return _proj(_ffn_prologue, ins, specs, w, x, g, seq=seq, tm=tm, tk=tk, final=final)


def _conv_silu_kernel(tm, seq, x_ref, xp_ref, xn_ref, w_ref, b_ref, o_ref):
    has_prev, has_next = _halo_flags(tm, seq, pl.program_id(0))
    x = x_ref[...].astype(F32)
    x_prev = jnp.where(has_prev, xp_ref[HALO - 1:HALO, :].astype(F32), 0.0)
    x_next = jnp.where(has_next, xn_ref[0:1, :].astype(F32), 0.0)
    y = _conv3_rows(x, x_prev, x_next, w_ref[...]) + b_ref[...]
    o_ref[...] = (y * jax.nn.sigmoid(y)).astype(o_ref.dtype)


def _conv_silu(zx, col_start, conv_w, conv_b, *, seq):
    m = zx.shape[0]
    ncols = conv_w.shape[1]
    tm = _pick(seq, (512, 256, 128))
    tc = _pick(ncols, (2048, 1024, 512, 256, 128))
    assert col_start % tc == 0
    col0 = col_start // tc
    xp, xn = _halo_specs(tm, tc, col0, m, _tile_pos)
    return pl.pallas_call(
        functools.partial(_conv_silu_kernel, tm, seq),
        grid=(m // tm, ncols // tc),
        in_specs=[pl.BlockSpec((tm, tc), lambda i, c: (i, col0 + c)), xp, xn,
                  pl.BlockSpec((3, tc), lambda i, c: (0, c)),
                  pl.BlockSpec((1, tc), lambda i, c: (0, c))],
        out_specs=pl.BlockSpec((tm, tc), lambda i, c: (i, c)),
        out_shape=jax.ShapeDtypeStruct((m, ncols), BF16),
        compiler_params=_params(2),
        name="ssd_conv_silu",
    )(zx, zx, zx, conv_w, conv_b.reshape(1, ncols))


def _cumsum_rows(a, reverse):
    n = a.shape[0]
    rid = lax.broadcasted_iota(jnp.int32, a.shape, 0)
    acc = a
    d = 1
    while d < n:
        if reverse:
            acc = acc + jnp.where(rid < n - d, pltpu.roll(acc, n - d, axis=0), 0.0)
        else:
            acc = acc + jnp.where(rid >= d, pltpu.roll(acc, d, axis=0), 0.0)
        d *= 2
    return acc


def _scan_kernel(reverse, n_heads, *refs):
    if reverse:
        (x_ref, bc_ref, dt_ref, dtb_ref, alog_ref, yf_ref, z_ref, ng_ref, o_ref, st_ref) = refs
    else:
        (x_ref, bc_ref, dt_ref, dtb_ref, alog_ref, dskip_ref, o_ref, st_ref) = refs
    L = CHUNK
    hg = n_heads // SSM_GROUPS
    gw = hg * SSM_HEAD_DIM
    pairs = hg // 2
    pw = 2 * SSM_HEAD_DIM

    @pl.when(pl.program_id(1) == 0)
    def _():
        st_ref[...] = jnp.zeros_like(st_ref)

    dt_raw = dt_ref[...] + dtb_ref[...]
    dt = jnp.maximum(dt_raw, 0.0) + jnp.log1p(jnp.exp(-jnp.abs(dt_raw)))
    a = dt * (-jnp.exp(alog_ref[...]))
    acs = _cumsum_rows(a, reverse)
    edge = acs[0:1, :] if reverse else acs[L - 1:L, :]
    eacs = jnp.exp(acs)
    wend = jnp.exp(edge - acs) * dt
    dtot = jnp.exp(edge)
    col2 = acs * LOG2_E
    row2_t = (col2 - jnp.log(dt) * LOG2_E).T
    wend_t = wend.T

    row = lax.broadcasted_iota(jnp.int32, (L, L), 0)
    col = lax.broadcasted_iota(jnp.int32, (L, L), 1)
    keep = (row <= col) if reverse else (row >= col)
    lane = lax.broadcasted_iota(jnp.int32, (L, pw), 1)
    first = lane < SSM_HEAD_DIM
    lane1 = lax.broadcasted_iota(jnp.int32, (1, pw), 1)
    first1 = lane1 < SSM_HEAD_DIM

    for g in range(SSM_GROUPS):
        b_g = bc_ref[:, g * D_STATE:(g + 1) * D_STATE]
        c_g = bc_ref[:, (SSM_GROUPS + g) * D_STATE:(SSM_GROUPS + g + 1) * D_STATE]
        cb = lax.dot_general(c_g, b_g, (((1,), (1,)), ((), ())),
                             preferred_element_type=F32)
        b_t = b_g.astype(F32).T
        z_all = jnp.dot(c_g, st_ref[g].astype(BF16), preferred_element_type=F32)
        ys = []
        for j in range(pairs):
            h0 = g * hg + 2 * j
            h1 = h0 + 1
            c0 = g * gw + j * pw
            xp = x_ref[:, c0:c0 + pw]
            zero = jnp.zeros_like(xp)
            rhs = jnp.concatenate([jnp.where(first, xp, zero),
                                   jnp.where(first, zero, xp)], axis=0)
            atts, bws = [], []
            for h in (h0, h1):
                seg = col2[:, h:h + 1] - row2_t[h:h + 1, :]
                dec = jnp.exp2(jnp.where(keep, seg, -jnp.inf))
                atts.append((cb * dec).astype(BF16))
                bws.append((b_t * wend_t[h:h + 1, :]).astype(BF16))
            y = jnp.dot(jnp.concatenate(atts, axis=1), rhs, preferred_element_type=F32)
            esc = jnp.where(first, eacs[:, h0:h0 + 1], eacs[:, h1:h1 + 1])
            y = y + z_all[:, j * pw:(j + 1) * pw] * esc
            s_old = st_ref[g, :, j * pw:(j + 1) * pw]
            s_dec = jnp.where(first1, dtot[:, h0:h0 + 1], dtot[:, h1:h1 + 1])
            st_ref[g, :, j * pw:(j + 1) * pw] = s_old * s_dec + jnp.dot(
                jnp.concatenate(bws, axis=1), rhs, preferred_element_type=F32)
            if reverse:
                ys.append(y)
            else:
                o_ref[:, c0:c0 + pw] = (y + dskip_ref[:, c0:c0 + pw] * xp.astype(F32)
                                        ).astype(o_ref.dtype)
        if reverse:
            sl = slice(g * gw, (g + 1) * gw)
            zz = z_ref[:, sl].astype(F32)
            u = (jnp.concatenate(ys, axis=1) + yf_ref[:, sl].astype(F32)) * (zz * jax.nn.sigmoid(zz))
            ms = jnp.mean(u * u, axis=-1, keepdims=True)
            o_ref[:, sl] = (u * lax.rsqrt(ms + SSM_NORM_EPS) * ng_ref[:, sl]).astype(o_ref.dtype)


def _ssd_scan(reverse, xbc, dt_raw, dt_bias, a_log, extra, *, batch, seq):
    m = xbc.shape[0]
    n_heads = dt_bias.shape[1]
    d_inner = n_heads * SSM_HEAD_DIM
    bc_w = 2 * SSM_GROUPS * D_STATE
    assert d_inner % bc_w == 0
    nc = seq // CHUNK
    L = CHUNK
    d = 1 if reverse else 0

    def rows(b, c):
        return b * nc + (nc - 1 - c if reverse else c)

    specs = [
        pl.BlockSpec((L, d_inner), lambda b, c: (rows(b, c), 0)),
        pl.BlockSpec((L, bc_w), lambda b, c: (rows(b, c), d_inner // bc_w)),
        pl.BlockSpec((L, n_heads), lambda b, c: (rows(b, c), d)),
        pl.BlockSpec((1, n_heads), lambda b, c: (0, 0)),
        pl.BlockSpec((1, n_heads), lambda b, c: (0, 0)),
    ]
    ins = [xbc, xbc, dt_raw, dt_bias[d:d + 1], a_log[d:d + 1]]
    if reverse:
        yf, zx, norm_g = extra
        specs += [pl.BlockSpec((L, d_inner), lambda b, c: (rows(b, c), 0)),
                  pl.BlockSpec((L, d_inner), lambda b, c: (rows(b, c), 0)),
                  pl.BlockSpec((1, d_inner), lambda b, c: (0, 0))]
        ins += [yf, zx, norm_g.reshape(1, d_inner)]
    else:
        (d_skip,) = extra
        specs += [pl.BlockSpec((1, d_inner), lambda b, c: (0, 0))]
        ins += [jnp.repeat(d_skip, SSM_HEAD_DIM).reshape(1, d_inner)]
    out_dtype = BF16
    gw = d_inner // SSM_GROUPS
    return pl.pallas_call(
        functools.partial(_scan_kernel, reverse, n_heads),
        grid=(batch, nc),
        in_specs=specs,
        out_specs=pl.BlockSpec((L, d_inner), lambda b, c: (rows(b, c), 0)),
        out_shape=jax.ShapeDtypeStruct((m, d_inner), out_dtype),
        scratch_shapes=[pltpu.VMEM((SSM_GROUPS, D_STATE, gw), F32)],
        compiler_params=_params(2),
        name="ssd_scan_bwd" if reverse else "ssd_scan_fwd",
    )(*ins)


def _xattn_kernel(h_ref, x_ref, wq_ref, kv_ref, wo_ref, g_ref, xo_ref, ho_ref):
    width = XA_HEADS * XA_HEAD_DIM
    q = jnp.dot(h_ref[...], wq_ref[...], preferred_element_type=F32).astype(BF16)
    outs = []
    for hd in range(XA_HEADS):
        sl = slice(hd * XA_HEAD_DIM, (hd + 1) * XA_HEAD_DIM)
        k_h = kv_ref[:, sl]
        v_h = kv_ref[:, width + hd * XA_HEAD_DIM: width + (hd + 1) * XA_HEAD_DIM]
        s = lax.dot_general(q[:, sl], k_h, (((1,), (1,)), ((), ())),
                            preferred_element_type=F32) * (XA_HEAD_DIM ** -0.5)
        p = jnp.exp(s - jnp.max(s, axis=-1, keepdims=True))
        denom = jnp.sum(p, axis=-1, keepdims=True)
        o_h = jnp.dot(p.astype(BF16), v_h, preferred_element_type=F32)
        outs.append(o_h / denom)
    o = jnp.concatenate(outs, axis=-1).astype(BF16)
    xn = x_ref[...] + jnp.dot(o, wo_ref[...], preferred_element_type=F32)
    xo_ref[...] = xn
    ms = jnp.mean(xn * xn, axis=-1, keepdims=True)
    ho_ref[...] = (xn * lax.rsqrt(ms + NORM_EPS) * g_ref[...]).astype(BF16)


def _xattn(h, x, wq, kv, wo, g, *, seq):
    m, d = x.shape
    n_mem, kvw = kv.shape[1], kv.shape[2]
    width = XA_HEADS * XA_HEAD_DIM
    tm = _pick(seq, (256, 128))
    row = pl.BlockSpec((tm, d), lambda i: (i, 0))
    return pl.pallas_call(
        _xattn_kernel,
        grid=(m // tm,),
        in_specs=[row, row,
                  pl.BlockSpec((d, width), lambda i: (0, 0)),
                  pl.BlockSpec((None, n_mem, kvw), lambda i: ((i * tm) // seq, 0, 0)),
                  pl.BlockSpec((width, d), lambda i: (0, 0)),
                  pl.BlockSpec((1, d), lambda i: (0, 0))],
        out_specs=(row, row),
        out_shape=(jax.ShapeDtypeStruct((m, d), F32), jax.ShapeDtypeStruct((m, d), BF16)),
        compiler_params=_params(1),
        name="xattn",
    )(h, x, wq, kv, wo, g.reshape(1, d))


def _trunk(x, mem, p):
    batch, seq, d = x.shape
    m = batch * seq
    n_mem = mem.shape[1]
    depth = p["g_mix"].shape[0]
    d_inner = p["ssm_norm_g"].shape[1]
    xf = x.reshape(m, d)
    memf = mem.reshape(batch * n_mem, d)
    h = _rmsnorm(xf, p["g_mix"][0])
    out = None
    for i in range(depth):
        j = i // 2
        if i % 2 == 0:
            bcv = _matmul(h, p["sc_w_in"][j])
            xf, h = _proj_sc(bcv, p["sc_conv_w"][j], p["sc_w_out"][j], xf, p["g_xattn"][i], seq=seq)
        else:
            zx = _matmul(h, p["ssm_w_zx"][j])
            dt_raw = _matmul(h, p["ssm_w_dt"][j], out_dtype=F32)
            xbc = _conv_silu(zx, d_inner, p["ssm_conv_w"][j], p["ssm_conv_b"][j], seq=seq)
            yf = _ssd_scan(False, xbc, dt_raw, p["ssm_dt_bias"][j], p["ssm_A_log"][j],
                           (p["ssm_D"][j],), batch=batch, seq=seq)
            a = _ssd_scan(True, xbc, dt_raw, p["ssm_dt_bias"][j], p["ssm_A_log"][j],
                          (yf, zx, p["ssm_norm_g"][j]), batch=batch, seq=seq)
            xf, h = _proj_plain(a, p["ssm_w_out"][j], xf, p["g_xattn"][i], seq=seq)
        kv = _matmul(_rmsnorm(memf, p["g_mem"][i]), p["xa_wkv"][i]).reshape(batch, n_mem, -1)
        xf, h = _xattn(h, xf, p["xa_wq"][i], kv, p["xa_wo"][i], p["g_ffn"][i], seq=seq)
        gv = _matmul(h, p["ffn_w_up"][i])
        if i + 1 < depth:
            xf, h = _proj_ffn(gv, p["ffn_conv_w"][i], p["ffn_conv_b"][i], p["ffn_w_down"][i],
                              xf, p["g_mix"][i + 1], seq=seq, final=False)
        else:
            out = _proj_ffn(gv, p["ffn_conv_w"][i], p["ffn_conv_b"][i], p["ffn_w_down"][i],
                            xf, p["g_final"], seq=seq, final=True)
    return out.reshape(batch, seq, d)


def _pad_to(a, axis, size):
    pad = [(0, 0)] * a.ndim
    pad[axis] = (0, size - a.shape[axis])
    return jnp.pad(a, pad)


def kernel(x_prompt, x_sample, mem_prompt, mem_sample, g_mix, g_xattn, g_mem, g_ffn, g_final, sc_w_in, sc_conv_w, sc_w_out, ssm_w_in, ssm_conv_w, ssm_conv_b, ssm_A_log, ssm_dt_bias, ssm_D, ssm_norm_g, ssm_w_out, xa_wq, xa_wk, xa_wv, xa_wo, ffn_w_up, ffn_conv_w, ffn_conv_b, ffn_w_down):
    d_inner = ssm_norm_g.shape[1]
    conv_dim = ssm_conv_w.shape[2]
    d_ff = ffn_w_down.shape[1]
    d_ff_pad = -(-d_ff // 512) * 512
    split = d_inner + conv_dim

    def per_layer(stacked, fn):
        return [fn(stacked[i]) for i in range(stacked.shape[0])]

    def bf16(w):
        return w.astype(BF16)

    p = {
        "g_mix": g_mix, "g_xattn": g_xattn, "g_mem": g_mem, "g_ffn": g_ffn, "g_final": g_final,
        "sc_w_in": per_layer(sc_w_in, bf16), "sc_conv_w": sc_conv_w,
        "sc_w_out": per_layer(sc_w_out, bf16),
        "ssm_w_zx": per_layer(ssm_w_in, lambda w: bf16(w[:, :split])),
        "ssm_w_dt": per_layer(ssm_w_in, lambda w: bf16(w[:, split:])),
        "ssm_conv_w": ssm_conv_w, "ssm_conv_b": ssm_conv_b,
        "ssm_A_log": ssm_A_log, "ssm_dt_bias": ssm_dt_bias, "ssm_D": ssm_D,
        "ssm_norm_g": ssm_norm_g, "ssm_w_out": per_layer(ssm_w_out, bf16),
        "xa_wq": per_layer(xa_wq, bf16),
        "xa_wkv": [jnp.concatenate([bf16(xa_wk[i]), bf16(xa_wv[i])], axis=-1)
                   for i in range(xa_wk.shape[0])],
        "xa_wo": per_layer(xa_wo, bf16),
        "ffn_w_up": per_layer(ffn_w_up, lambda w: jnp.concatenate(
            [_pad_to(bf16(w[:, :d_ff]), 1, d_ff_pad), _pad_to(bf16(w[:, d_ff:]), 1, d_ff_pad)], axis=-1)),
        "ffn_conv_w": _pad_to(ffn_conv_w, 2, d_ff_pad),
        "ffn_conv_b": _pad_to(ffn_conv_b, 1, d_ff_pad),
        "ffn_w_down": per_layer(ffn_w_down, lambda w: _pad_to(bf16(w), 0, d_ff_pad)),
    }
    return (_trunk(x_prompt, mem_prompt, p), _trunk(x_sample, mem_sample, p))
```

```python
import functools

import jax
import jax.numpy as jnp
from jax import lax
from jax.experimental import pallas as pl
from jax.experimental.pallas import tpu as pltpu

F32 = jnp.float32
BF16 = jnp.bfloat16

NORM_EPS = 1e-6
SSM_NORM_EPS = 1e-5
SSM_HEAD_DIM = 64
SSM_GROUPS = 8
D_STATE = 128
CHUNK = 128
XA_HEADS = 4
XA_HEAD_DIM = 128
LOG2_E = 1.4426950408889634

LANES = 128
F32_SUBLANES = 8
BF16_SUBLANES = 16
PROJ_N_CHUNK = 512
MATMUL_N_CHUNK = 1024
HALO = BF16_SUBLANES
V7X_VMEM_BYTES = 64 * 1024 * 1024
VMEM_LIMIT = V7X_VMEM_BYTES - 4 * 1024 * 1024


def _params(n_axes):
    return pltpu.CompilerParams(
        dimension_semantics=("arbitrary",) * n_axes,
        vmem_limit_bytes=VMEM_LIMIT)


def _pick(n, prefs):
    for p in prefs:
        if n % p == 0:
            return p
    return n


def _rmsnorm_kernel(x_ref, g_ref, o_ref):
    x = x_ref[...]
    ms = jnp.mean(x * x, axis=-1, keepdims=True)
    o_ref[...] = (x * lax.rsqrt(ms + NORM_EPS) * g_ref[...]).astype(o_ref.dtype)


def _rmsnorm(x, g):
    m, d = x.shape
    tm = _pick(m, (512, 256, 128))
    return pl.pallas_call(
        _rmsnorm_kernel,
        grid=(m // tm,),
        in_specs=[pl.BlockSpec((tm, d), lambda i: (i, 0)),
                  pl.BlockSpec((1, d), lambda i: (0, 0))],
        out_specs=pl.BlockSpec((tm, d), lambda i: (i, 0)),
        out_shape=jax.ShapeDtypeStruct((m, d), BF16),
        compiler_params=_params(1),
        name="rmsnorm",
    )(x, g.reshape(1, d))


def _matmul_kernel(a_ref, w_ref, o_ref):
    nchunk = _pick(o_ref.shape[1], (MATMUL_N_CHUNK, LANES))
    for n0 in range(0, o_ref.shape[1], nchunk):
        o_ref[:, n0:n0 + nchunk] = jnp.dot(a_ref[...], w_ref[:, n0:n0 + nchunk],
                                           preferred_element_type=F32).astype(o_ref.dtype)


def _w_spec(w, layer, rows, cols, index):
    if w.ndim == 2:
        return pl.BlockSpec((rows, cols), index)
    return pl.BlockSpec((None, rows, cols), lambda *idx: (layer,) + tuple(index(*idx)))


def _matmul(a, w, out_dtype=BF16, layer=None, col=0, n=None):
    m, k = a.shape
    n = w.shape[-1] if n is None else n
    tm = _pick(m, (1024, 512, 256, 128))
    tn = _pick(n, (1024, 512, 256, 128))
    assert col % tn == 0
    c0 = col // tn
    return pl.pallas_call(
        _matmul_kernel,
        grid=(m // tm, n // tn),
        in_specs=[pl.BlockSpec((tm, k), lambda i, j: (i, 0)),
                  _w_spec(w, layer, k, tn, lambda i, j: (0, c0 + j))],
        out_specs=pl.BlockSpec((tm, tn), lambda i, j: (i, j)),
        out_shape=jax.ShapeDtypeStruct((m, n), out_dtype),
        compiler_params=_params(2),
        name="matmul",
    )(a, w)


def _conv3_rows(u, u_prev, u_next, w):
    t = u.shape[0]
    rid = lax.broadcasted_iota(jnp.int32, u.shape, 0)
    below = jnp.where(rid == 0, u_prev, pltpu.roll(u, 1, axis=0))
    above = jnp.where(rid == t - 1, u_next, pltpu.roll(u, t - 1, axis=0))
    return below * w[0:1, :] + u * w[1:2, :] + above * w[2:3, :]


def _halo_flags(tm, seq, i):
    has_prev = (i * tm) % seq != 0
    has_next = ((i + 1) * tm) % seq != 0
    return has_prev, has_next


def _halo_specs(tm, tc, col0, m, pos):
    per = tm // HALO
    last = m // HALO - 1

    def prev(*idx):
        i, c = pos(*idx)
        return jnp.maximum(i * per - 1, 0), col0 + c

    def nxt(*idx):
        i, c = pos(*idx)
        return jnp.minimum((i + 1) * per, last), col0 + c

    return pl.BlockSpec((HALO, tc), prev), pl.BlockSpec((HALO, tc), nxt)


def _sc_prologue(tm, seq, i, refs):
    b_ref, c_ref, v_ref, cp_ref, cn_ref, vp_ref, vn_ref, w_ref = refs
    has_prev, has_next = _halo_flags(tm, seq, i)
    u = c_ref[...].astype(F32) * v_ref[...].astype(F32)
    u_prev = cp_ref[HALO - 1:HALO, :].astype(F32) * vp_ref[HALO - 1:HALO, :].astype(F32)
    u_next = cn_ref[0:1, :].astype(F32) * vn_ref[0:1, :].astype(F32)
    u_prev = jnp.where(has_prev, u_prev, 0.0)
    u_next = jnp.where(has_next, u_next, 0.0)
    y = _conv3_rows(u, u_prev, u_next, w_ref[...])
    return (b_ref[...].astype(F32) * y).astype(BF16)


def _ffn_prologue(tm, seq, i, refs):
    g_ref, v_ref, gp_ref, gn_ref, w_ref, b_ref = refs
    has_prev, has_next = _halo_flags(tm, seq, i)
    g = g_ref[...].astype(F32)
    g_prev = jnp.where(has_prev, gp_ref[HALO - 1:HALO, :].astype(F32), 0.0)
    g_next = jnp.where(has_next, gn_ref[0:1, :].astype(F32), 0.0)
    y = _conv3_rows(g, g_prev, g_next, w_ref[...]) + b_ref[...]
    return (y * jax.nn.sigmoid(y) * v_ref[...].astype(F32)).astype(BF16)


def _proj_kernel(prologue, n_in, nk, n_slabs, final, tm, seq, *refs):
    in_refs = refs[:n_in]
    w_ref, x_ref, g_ref = refs[n_in:n_in + 3]
    out_refs = refs[n_in + 3:]
    acc_ref = out_refs[0]
    i = pl.program_id(0)
    k = pl.program_id(1)
    d = acc_ref.shape[1]
    nchunk = _pick(d, (PROJ_N_CHUNK, LANES))

    def step(first):
        a = in_refs[0][...] if prologue is None else prologue(tm, seq, i, in_refs)
        for n0 in range(0, d, nchunk):
            part = jnp.dot(a, w_ref[:, n0:n0 + nchunk], preferred_element_type=F32)
            if first:
                acc_ref[:, n0:n0 + nchunk] = part
            else:
                acc_ref[:, n0:n0 + nchunk] += part

    pl.when(k == 0)(functools.partial(step, True))
    pl.when(k > 0)(functools.partial(step, False))

    rows = tm // n_slabs

    @pl.when(k < n_slabs)
    def _():
        r0 = pl.multiple_of(k * rows, rows)
        acc_ref[pl.ds(r0, rows), :] += x_ref[...]

    @pl.when(k == nk - 1)
    def _():
        xn = acc_ref[...]
        ms = jnp.mean(xn * xn, axis=-1, keepdims=True)
        hn = xn * lax.rsqrt(ms + NORM_EPS) * g_ref[...]
        if final:
            acc_ref[...] = hn
        else:
            out_refs[1][...] = hn.astype(BF16)


def _proj(prologue, ins, in_specs, w, layer, x, g, *, seq, tm, tk, final):
    m, d = x.shape
    nk = w.shape[-2] // tk
    n_slabs = max(s for s in range(1, min(nk, 8) + 1) if tm % (s * F32_SUBLANES) == 0)
    rows = tm // n_slabs
    kern = functools.partial(_proj_kernel, prologue, len(ins), nk, n_slabs, final, tm, seq)
    specs = list(in_specs) + [
        _w_spec(w, layer, tk, d, lambda i, k: (k, 0)),
        pl.BlockSpec((rows, d), lambda i, k: (i * n_slabs + jnp.minimum(k, n_slabs - 1), 0)),
        pl.BlockSpec((1, d), lambda i, k: (0, 0)),
    ]
    x_spec = pl.BlockSpec((tm, d), lambda i, k: (i, 0))
    if final:
        out_specs = x_spec
        out_shape = jax.ShapeDtypeStruct((m, d), F32)
    else:
        out_specs = (x_spec, pl.BlockSpec((tm, d), lambda i, k: (i, 0)))
        out_shape = (jax.ShapeDtypeStruct((m, d), F32), jax.ShapeDtypeStruct((m, d), BF16))
    return pl.pallas_call(
        kern,
        grid=(m // tm, nk),
        in_specs=specs,
        out_specs=out_specs,
        out_shape=out_shape,
        compiler_params=_params(2),
        name="out_proj",
    )(*ins, w, x, g.reshape(1, d))


def _proj_tiles(seq, kdim, tk_max=1024):
    tks = tuple(t for t in (1024, 512, 256, 128) if t <= tk_max)
    return _pick(seq, (512, 256, 128)), _pick(kdim, tks)


def _proj_plain(a, w, layer, x, g, *, seq):
    tm, tk = _proj_tiles(seq, w.shape[-2])
    specs = [pl.BlockSpec((tm, tk), lambda i, k: (i, k))]
    return _proj(None, [a], specs, w, layer, x, g, seq=seq, tm=tm, tk=tk, final=False)


def _tile_pos(i, k):
    return i, k


def _proj_sc(bcv, conv_w, w, layer, x, g, *, seq):
    m, d = x.shape
    tm, tk = _proj_tiles(seq, d, tk_max=512)
    nb = d // tk
    main = lambda off: pl.BlockSpec((tm, tk), lambda i, k: (i, off + k))
    cp, cn = _halo_specs(tm, tk, nb, m, _tile_pos)
    vp, vn = _halo_specs(tm, tk, 2 * nb, m, _tile_pos)
    specs = [main(0), main(nb), main(2 * nb), cp, cn, vp, vn,
             pl.BlockSpec((3, tk), lambda i, k: (0, k))]
    ins = [bcv, bcv, bcv, bcv, bcv, bcv, bcv, conv_w]
    return _proj(_sc_prologue, ins, specs, w, layer, x, g, seq=seq, tm=tm, tk=tk, final=False)


def _proj_ffn(gv, conv_w, conv_b, w, layer, x, g, *, seq, final):
    m, d = x.shape
    dff = w.shape[-2]
    tm, tk = _proj_tiles(seq, dff)
    nb = dff // tk
    main = lambda off: pl.BlockSpec((tm, tk), lambda i, k: (i, off + k))
    gp, gn = _halo_specs(tm, tk, 0, m, _tile_pos)
    specs = [main(0), main(nb), gp, gn,
             pl.BlockSpec((3, tk), lambda i, k: (0, k)),
             pl.BlockSpec((1, tk), lambda i, k: (0, k))]
    ins = [gv, gv, gv, gv, conv_w, conv_b.reshape(1, dff)]
    return _proj(_ffn_prologue, ins, specs, w, layer, x, g, seq=seq, tm=tm, tk=tk, final=final)


def _conv_silu_kernel(tm, seq, x_ref, xp_ref, xn_ref, w_ref, b_ref, o_ref):
    has_prev, has_next = _halo_flags(tm, seq, pl.program_id(0))
    x = x_ref[...].astype(F32)
    x_prev = jnp.where(has_prev, xp_ref[HALO - 1:HALO, :].astype(F32), 0.0)
    x_next = jnp.where(has_next, xn_ref[0:1, :].astype(F32), 0.0)
    y = _conv3_rows(x, x_prev, x_next, w_ref[...]) + b_ref[...]
    o_ref[...] = (y * jax.nn.sigmoid(y)).astype(o_ref.dtype)


def _conv_silu(zx, col_start, conv_w, conv_b, *, seq):
    m = zx.shape[0]
    ncols = conv_w.shape[1]
    tm = _pick(seq, (512, 256, 128))
    tc = _pick(ncols, (2048, 1024, 512, 256, 128))
    assert col_start % tc == 0
    col0 = col_start // tc
    xp, xn = _halo_specs(tm, tc, col0, m, _tile_pos)
    return pl.pallas_call(
        functools.partial(_conv_silu_kernel, tm, seq),
        grid=(m // tm, ncols // tc),
        in_specs=[pl.BlockSpec((tm, tc), lambda i, c: (i, col0 + c)), xp, xn,
                  pl.BlockSpec((3, tc), lambda i, c: (0, c)),
                  pl.BlockSpec((1, tc), lambda i, c: (0, c))],
        out_specs=pl.BlockSpec((tm, tc), lambda i, c: (i, c)),
        out_shape=jax.ShapeDtypeStruct((m, ncols), BF16),
        compiler_params=_params(2),
        name="ssd_conv_silu",
    )(zx, zx, zx, conv_w, conv_b.reshape(1, ncols))


def _cumsum_rows(a, reverse):
    n = a.shape[0]
    rid = lax.broadcasted_iota(jnp.int32, a.shape, 0)
    acc = a
    d = 1
    while d < n:
        if reverse:
            acc = acc + jnp.where(rid < n - d, pltpu.roll(acc, n - d, axis=0), 0.0)
        else:
            acc = acc + jnp.where(rid >= d, pltpu.roll(acc, d, axis=0), 0.0)
        d *= 2
    return acc


def _scan_kernel(reverse, n_heads, *refs):
    if reverse:
        (x_ref, bc_ref, dt_ref, dtb_ref, alog_ref, yf_ref, z_ref, ng_ref, o_ref, st_ref) = refs
    else:
        (x_ref, bc_ref, dt_ref, dtb_ref, alog_ref, dskip_ref, o_ref, st_ref) = refs
    L = CHUNK
    hg = n_heads // SSM_GROUPS
    gw = hg * SSM_HEAD_DIM
    pairs = hg // 2
    pw = 2 * SSM_HEAD_DIM

    @pl.when(pl.program_id(1) == 0)
    def _():
        st_ref[...] = jnp.zeros_like(st_ref)

    dt_raw = dt_ref[...] + dtb_ref[...]
    dt = jnp.maximum(dt_raw, 0.0) + jnp.log1p(jnp.exp(-jnp.abs(dt_raw)))
    a = dt * (-jnp.exp(alog_ref[...]))
    acs = _cumsum_rows(a, reverse)
    edge = acs[0:1, :] if reverse else acs[L - 1:L, :]
    eacs = jnp.exp(acs)
    wend = jnp.exp(edge - acs) * dt
    dtot = jnp.exp(edge)
    col2 = acs * LOG2_E
    row2_t = (col2 - jnp.log(dt) * LOG2_E).T
    wend_t = wend.T

    row = lax.broadcasted_iota(jnp.int32, (L, L), 0)
    col = lax.broadcasted_iota(jnp.int32, (L, L), 1)
    keep = (row <= col) if reverse else (row >= col)
    lane = lax.broadcasted_iota(jnp.int32, (L, pw), 1)
    first = lane < SSM_HEAD_DIM
    lane1 = lax.broadcasted_iota(jnp.int32, (1, pw), 1)
    first1 = lane1 < SSM_HEAD_DIM

    for g in range(SSM_GROUPS):
        b_g = bc_ref[:, g * D_STATE:(g + 1) * D_STATE]
        c_g = bc_ref[:, (SSM_GROUPS + g) * D_STATE:(SSM_GROUPS + g + 1) * D_STATE]
        cb = lax.dot_general(c_g, b_g, (((1,), (1,)), ((), ())),
                             preferred_element_type=F32)
        b_t = b_g.astype(F32).T
        z_all = jnp.dot(c_g, st_ref[g].astype(BF16), preferred_element_type=F32)
        ys = []
        for j in range(pairs):
            h0 = g * hg + 2 * j
            h1 = h0 + 1
            c0 = g * gw + j * pw
            xp = x_ref[:, c0:c0 + pw]
            zero = jnp.zeros_like(xp)
            rhs = jnp.concatenate([jnp.where(first, xp, zero),
                                   jnp.where(first, zero, xp)], axis=0)
            atts, bws = [], []
            for h in (h0, h1):
                seg = col2[:, h:h + 1] - row2_t[h:h + 1, :]
                dec = jnp.exp2(jnp.where(keep, seg, -jnp.inf))
                atts.append((cb * dec).astype(BF16))
                bws.append((b_t * wend_t[h:h + 1, :]).astype(BF16))
            y = jnp.dot(jnp.concatenate(atts, axis=1), rhs, preferred_element_type=F32)
            esc = jnp.where(first, eacs[:, h0:h0 + 1], eacs[:, h1:h1 + 1])
            y = y + z_all[:, j * pw:(j + 1) * pw] * esc
            s_old = st_ref[g, :, j * pw:(j + 1) * pw]
            s_dec = jnp.where(first1, dtot[:, h0:h0 + 1], dtot[:, h1:h1 + 1])
            st_ref[g, :, j * pw:(j + 1) * pw] = s_old * s_dec + jnp.dot(
                jnp.concatenate(bws, axis=1), rhs, preferred_element_type=F32)
            if reverse:
                ys.append(y)
            else:
                o_ref[:, c0:c0 + pw] = (y + dskip_ref[:, c0:c0 + pw] * xp.astype(F32)
                                        ).astype(o_ref.dtype)
        if reverse:
            sl = slice(g * gw, (g + 1) * gw)
            zz = z_ref[:, sl].astype(F32)
            u = (jnp.concatenate(ys, axis=1) + yf_ref[:, sl].astype(F32)) * (zz * jax.nn.sigmoid(zz))
            ms = jnp.mean(u * u, axis=-1, keepdims=True)
            o_ref[:, sl] = (u * lax.rsqrt(ms + SSM_NORM_EPS) * ng_ref[:, sl]).astype(o_ref.dtype)


def _ssd_scan(reverse, xbc, dt_raw, dt_bias, a_log, extra, *, batch, seq):
    m = xbc.shape[0]
    n_heads = dt_bias.shape[1]
    d_inner = n_heads * SSM_HEAD_DIM
    bc_w = 2 * SSM_GROUPS * D_STATE
    assert d_inner % bc_w == 0
    nc = seq // CHUNK
    L = CHUNK
    d = 1 if reverse else 0

    def rows(b, c):
        return b * nc + (nc - 1 - c if reverse else c)

    specs = [
        pl.BlockSpec((L, d_inner), lambda b, c: (rows(b, c), 0)),
        pl.BlockSpec((L, bc_w), lambda b, c: (rows(b, c), d_inner // bc_w)),
        pl.BlockSpec((L, n_heads), lambda b, c: (rows(b, c), d)),
        pl.BlockSpec((1, n_heads), lambda b, c: (0, 0)),
        pl.BlockSpec((1, n_heads), lambda b, c: (0, 0)),
    ]
    ins = [xbc, xbc, dt_raw, dt_bias[d:d + 1], a_log[d:d + 1]]
    if reverse:
        yf, zx, norm_g = extra
        specs += [pl.BlockSpec((L, d_inner), lambda b, c: (rows(b, c), 0)),
                  pl.BlockSpec((L, d_inner), lambda b, c: (rows(b, c), 0)),
                  pl.BlockSpec((1, d_inner), lambda b, c: (0, 0))]
        ins += [yf, zx, norm_g.reshape(1, d_inner)]
    else:
        (d_skip,) = extra
        specs += [pl.BlockSpec((1, d_inner), lambda b, c: (0, 0))]
        ins += [jnp.repeat(d_skip, SSM_HEAD_DIM).reshape(1, d_inner)]
    out_dtype = BF16
    gw = d_inner // SSM_GROUPS
    return pl.pallas_call(
        functools.partial(_scan_kernel, reverse, n_heads),
        grid=(batch, nc),
        in_specs=specs,
        out_specs=pl.BlockSpec((L, d_inner), lambda b, c: (rows(b, c), 0)),
        out_shape=jax.ShapeDtypeStruct((m, d_inner), out_dtype),
        scratch_shapes=[pltpu.VMEM((SSM_GROUPS, D_STATE, gw), F32)],
        compiler_params=_params(2),
        name="ssd_scan_bwd" if reverse else "ssd_scan_fwd",
    )(*ins)


def _xattn_kernel(h_ref, x_ref, wq_ref, kv_ref, wo_ref, g_ref, xo_ref, ho_ref):
    width = XA_HEADS * XA_HEAD_DIM
    q = jnp.dot(h_ref[...], wq_ref[...], preferred_element_type=F32).astype(BF16)
    outs = []
    for hd in range(XA_HEADS):
        sl = slice(hd * XA_HEAD_DIM, (hd + 1) * XA_HEAD_DIM)
        k_h = kv_ref[:, sl]
        v_h = kv_ref[:, width + hd * XA_HEAD_DIM: width + (hd + 1) * XA_HEAD_DIM]
        s = lax.dot_general(q[:, sl], k_h, (((1,), (1,)), ((), ())),
                            preferred_element_type=F32) * (XA_HEAD_DIM ** -0.5)
        p = jnp.exp(s - jnp.max(s, axis=-1, keepdims=True))
        denom = jnp.sum(p, axis=-1, keepdims=True)
        o_h = jnp.dot(p.astype(BF16), v_h, preferred_element_type=F32)
        outs.append(o_h / denom)
    o = jnp.concatenate(outs, axis=-1).astype(BF16)
    xn = x_ref[...] + jnp.dot(o, wo_ref[...], preferred_element_type=F32)
    xo_ref[...] = xn
    ms = jnp.mean(xn * xn, axis=-1, keepdims=True)
    ho_ref[...] = (xn * lax.rsqrt(ms + NORM_EPS) * g_ref[...]).astype(BF16)


def _xattn(h, x, wq, kv, wo, layer, g, *, seq):
    m, d = x.shape
    n_mem, kvw = kv.shape[1], kv.shape[2]
    width = XA_HEADS * XA_HEAD_DIM
    tm = _pick(seq, (256, 128))
    row = pl.BlockSpec((tm, d), lambda i: (i, 0))
    return pl.pallas_call(
        _xattn_kernel,
        grid=(m // tm,),
        in_specs=[row, row,
                  _w_spec(wq, layer, d, width, lambda i: (0, 0)),
                  pl.BlockSpec((None, n_mem, kvw), lambda i: ((i * tm) // seq, 0, 0)),
                  _w_spec(wo, layer, width, d, lambda i: (0, 0)),
                  pl.BlockSpec((1, d), lambda i: (0, 0))],
        out_specs=(row, row),
        out_shape=(jax.ShapeDtypeStruct((m, d), F32), jax.ShapeDtypeStruct((m, d), BF16)),
        compiler_params=_params(1),
        name="xattn",
    )(h, x, wq, kv, wo, g.reshape(1, d))


def _trunk(x, mem, p):
    batch, seq, d = x.shape
    m = batch * seq
    n_mem = mem.shape[1]
    depth = p["g_mix"].shape[0]
    d_inner = p["ssm_norm_g"].shape[1]
    xf = x.reshape(m, d)
    memf = mem.reshape(batch * n_mem, d)
    h = _rmsnorm(xf, p["g_mix"][0])
    out = None
    for i in range(depth):
        j = i // 2
        if i % 2 == 0:
            bcv = _matmul(h, p["sc_w_in"], layer=j)
            xf, h = _proj_sc(bcv, p["sc_conv_w"][j], p["sc_w_out"], j, xf, p["g_xattn"][i], seq=seq)
        else:
            w_in = p["ssm_w_in"]
            split = d_inner + p["ssm_conv_w"].shape[2]
            zx = _matmul(h, w_in, layer=j, n=split)
            dt_raw = _matmul(h, w_in, out_dtype=F32, layer=j, col=split, n=w_in.shape[2] - split)
            xbc = _conv_silu(zx, d_inner, p["ssm_conv_w"][j], p["ssm_conv_b"][j], seq=seq)
            yf = _ssd_scan(False, xbc, dt_raw, p["ssm_dt_bias"][j], p["ssm_A_log"][j],
                           (p["ssm_D"][j],), batch=batch, seq=seq)
            a = _ssd_scan(True, xbc, dt_raw, p["ssm_dt_bias"][j], p["ssm_A_log"][j],
                          (yf, zx, p["ssm_norm_g"][j]), batch=batch, seq=seq)
            xf, h = _proj_plain(a, p["ssm_w_out"], j, xf, p["g_xattn"][i], seq=seq)
        kv = _matmul(_rmsnorm(memf, p["g_mem"][i]), p["xa_wkv"], layer=i).reshape(batch, n_mem, -1)
        xf, h = _xattn(h, xf, p["xa_wq"], kv, p["xa_wo"], i, p["g_ffn"][i], seq=seq)
        gv = _matmul(h, p["ffn_w_up"], layer=i)
        if i + 1 < depth:
            xf, h = _proj_ffn(gv, p["ffn_conv_w"][i], p["ffn_conv_b"][i], p["ffn_w_down"], i,
                              xf, p["g_mix"][i + 1], seq=seq, final=False)
        else:
            out = _proj_ffn(gv, p["ffn_conv_w"][i], p["ffn_conv_b"][i], p["ffn_w_down"], i,
                            xf, p["g_final"], seq=seq, final=True)
    return out.reshape(batch, seq, d)


def _pad_to(a, axis, size):
    pad = [(0, 0)] * a.ndim
    pad[axis] = (0, size - a.shape[axis])
    return jnp.pad(a, pad)


def kernel(x_prompt, x_sample, mem_prompt, mem_sample, g_mix, g_xattn, g_mem, g_ffn, g_final, sc_w_in, sc_conv_w, sc_w_out, ssm_w_in, ssm_conv_w, ssm_conv_b, ssm_A_log, ssm_dt_bias, ssm_D, ssm_norm_g, ssm_w_out, xa_wq, xa_wk, xa_wv, xa_wo, ffn_w_up, ffn_conv_w, ffn_conv_b, ffn_w_down):
    d_ff = ffn_w_down.shape[1]
    d_ff_pad = -(-d_ff // 1024) * 1024

    def bf16(w):
        return w.astype(BF16)

    p = {
        "g_mix": g_mix, "g_xattn": g_xattn, "g_mem": g_mem, "g_ffn": g_ffn, "g_final": g_final,
        "sc_w_in": bf16(sc_w_in), "sc_conv_w": sc_conv_w, "sc_w_out": bf16(sc_w_out),
        "ssm_w_in": bf16(ssm_w_in), "ssm_conv_w": ssm_conv_w, "ssm_conv_b": ssm_conv_b,
        "ssm_A_log": ssm_A_log, "ssm_dt_bias": ssm_dt_bias, "ssm_D": ssm_D,
        "ssm_norm_g": ssm_norm_g, "ssm_w_out": bf16(ssm_w_out),
        "xa_wq": bf16(xa_wq),
        "xa_wkv": jnp.concatenate([bf16(xa_wk), bf16(xa_wv)], axis=-1),
        "xa_wo": bf16(xa_wo),
        "ffn_w_up": jnp.concatenate([_pad_to(bf16(ffn_w_up[:, :, :d_ff]), 2, d_ff_pad),
                                     _pad_to(bf16(ffn_w_up[:, :, d_ff:]), 2, d_ff_pad)], axis=-1),
        "ffn_conv_w": _pad_to(ffn_conv_w, 2, d_ff_pad),
        "ffn_conv_b": _pad_to(ffn_conv_b, 1, d_ff_pad),
        "ffn_w_down": _pad_to(bf16(ffn_w_down), 1, d_ff_pad),
    }
    return (_trunk(x_prompt, mem_prompt, p), _trunk(x_sample, mem_sample, p))
```

```python
import functools

import jax
import jax.numpy as jnp
from jax import lax
from jax.experimental import pallas as pl
from jax.experimental.pallas import tpu as pltpu

F32 = jnp.float32
BF16 = jnp.bfloat16

NORM_EPS = 1e-6
SSM_NORM_EPS = 1e-5
SSM_HEAD_DIM = 64
SSM_GROUPS = 8
D_STATE = 128
CHUNK = 128
XA_HEADS = 4
XA_HEAD_DIM = 128
LOG2_E = 1.4426950408889634

LANES = 128
F32_SUBLANES = 8
BF16_SUBLANES = 16
PROJ_N_CHUNK = 512
MATMUL_N_CHUNK = 1024
HALO = BF16_SUBLANES
V7X_VMEM_BYTES = 64 * 1024 * 1024
VMEM_LIMIT = V7X_VMEM_BYTES - 4 * 1024 * 1024


def _params(n_axes):
    return pltpu.CompilerParams(
        dimension_semantics=("arbitrary",) * n_axes,
        vmem_limit_bytes=VMEM_LIMIT)


def _pick(n, prefs):
    for p in prefs:
        if n % p == 0:
            return p
    return n


def _rmsnorm_kernel(x_ref, g_ref, o_ref):
    x = x_ref[...]
    ms = jnp.mean(x * x, axis=-1, keepdims=True)
    o_ref[...] = (x * lax.rsqrt(ms + NORM_EPS) * g_ref[...]).astype(o_ref.dtype)


def _rmsnorm(x, g):
    m, d = x.shape
    tm = _pick(m, (512, 256, 128))
    return pl.pallas_call(
        _rmsnorm_kernel,
        grid=(m // tm,),
        in_specs=[pl.BlockSpec((tm, d), lambda i: (i, 0)),
                  pl.BlockSpec((1, d), lambda i: (0, 0))],
        out_specs=pl.BlockSpec((tm, d), lambda i: (i, 0)),
        out_shape=jax.ShapeDtypeStruct((m, d), BF16),
        compiler_params=_params(1),
        name="rmsnorm",
    )(x, g.reshape(1, d))


def _matmul_kernel(a_ref, w_ref, o_ref):
    nchunk = _pick(o_ref.shape[1], (MATMUL_N_CHUNK, LANES))
    for n0 in range(0, o_ref.shape[1], nchunk):
        o_ref[:, n0:n0 + nchunk] = jnp.dot(a_ref[...], w_ref[:, n0:n0 + nchunk],
                                           preferred_element_type=F32).astype(o_ref.dtype)


def _w_spec(w, layer, rows, cols, index):
    if w.ndim == 2:
        return pl.BlockSpec((rows, cols), index)
    return pl.BlockSpec((None, rows, cols), lambda *idx: (layer,) + tuple(index(*idx)))


def _matmul(a, w, out_dtype=BF16, layer=None, col=0, n=None):
    m, k = a.shape
    n = w.shape[-1] if n is None else n
    tm = _pick(m, (1024, 512, 256, 128))
    tn = _pick(n, (1024, 512, 256, 128))
    assert col % tn == 0
    c0 = col // tn
    return pl.pallas_call(
        _matmul_kernel,
        grid=(m // tm, n // tn),
        in_specs=[pl.BlockSpec((tm, k), lambda i, j: (i, 0)),
                  _w_spec(w, layer, k, tn, lambda i, j: (0, c0 + j))],
        out_specs=pl.BlockSpec((tm, tn), lambda i, j: (i, j)),
        out_shape=jax.ShapeDtypeStruct((m, n), out_dtype),
        compiler_params=_params(2),
        name="matmul",
    )(a, w)


def _conv3_rows(u, u_prev, u_next, w):
    t = u.shape[0]
    rid = lax.broadcasted_iota(jnp.int32, u.shape, 0)
    below = jnp.where(rid == 0, u_prev, pltpu.roll(u, 1, axis=0))
    above = jnp.where(rid == t - 1, u_next, pltpu.roll(u, t - 1, axis=0))
    return below * w[0:1, :] + u * w[1:2, :] + above * w[2:3, :]


def _halo_flags(tm, seq, i):
    has_prev = (i * tm) % seq != 0
    has_next = ((i + 1) * tm) % seq != 0
    return has_prev, has_next


def _halo_specs(tm, tc, col0, m, pos):
    per = tm // HALO
    last = m // HALO - 1

    def prev(*idx):
        i, c = pos(*idx)
        return jnp.maximum(i * per - 1, 0), col0 + c

    def nxt(*idx):
        i, c = pos(*idx)
        return jnp.minimum((i + 1) * per, last), col0 + c

    return pl.BlockSpec((HALO, tc), prev), pl.BlockSpec((HALO, tc), nxt)


def _sc_prologue(tm, seq, i, refs):
    b_ref, c_ref, v_ref, cp_ref, cn_ref, vp_ref, vn_ref, w_ref = refs
    has_prev, has_next = _halo_flags(tm, seq, i)
    u = c_ref[...].astype(F32) * v_ref[...].astype(F32)
    u_prev = cp_ref[HALO - 1:HALO, :].astype(F32) * vp_ref[HALO - 1:HALO, :].astype(F32)
    u_next = cn_ref[0:1, :].astype(F32) * vn_ref[0:1, :].astype(F32)
    u_prev = jnp.where(has_prev, u_prev, 0.0)
    u_next = jnp.where(has_next, u_next, 0.0)
    y = _conv3_rows(u, u_prev, u_next, w_ref[...])
    return (b_ref[...].astype(F32) * y).astype(BF16)


def _ffn_prologue(tm, seq, i, refs):
    g_ref, v_ref, gp_ref, gn_ref, w_ref, b_ref = refs
    has_prev, has_next = _halo_flags(tm, seq, i)
    g = g_ref[...].astype(F32)
    g_prev = jnp.where(has_prev, gp_ref[HALO - 1:HALO, :].astype(F32), 0.0)
    g_next = jnp.where(has_next, gn_ref[0:1, :].astype(F32), 0.0)
    y = _conv3_rows(g, g_prev, g_next, w_ref[...]) + b_ref[...]
    return (y * jax.nn.sigmoid(y) * v_ref[...].astype(F32)).astype(BF16)


def _proj_kernel(prologue, n_in, nk, n_slabs, n_cols, final, tm, seq, *refs):
    in_refs = refs[:n_in]
    w_ref, x_ref, g_ref = refs[n_in:n_in + 3]
    n_out = 1 if final else 2
    out_refs = refs[n_in + 3:n_in + 3 + n_out]
    acc_ref, rs_ref = refs[n_in + 3 + n_out:]
    i = pl.program_id(0)
    k = pl.program_id(1)
    d = acc_ref.shape[1]
    nchunk = _pick(d, (PROJ_N_CHUNK, LANES))

    def step(first):
        a = in_refs[0][...] if prologue is None else prologue(tm, seq, i, in_refs)
        for n0 in range(0, d, nchunk):
            part = jnp.dot(a, w_ref[:, n0:n0 + nchunk], preferred_element_type=F32)
            if first:
                acc_ref[:, n0:n0 + nchunk] = part
            else:
                acc_ref[:, n0:n0 + nchunk] += part

    pl.when(k == 0)(functools.partial(step, True))
    pl.when(jnp.logical_and(k > 0, k < nk))(functools.partial(step, False))

    rows = tm // n_slabs

    @pl.when(k < n_slabs)
    def _():
        r0 = pl.multiple_of(k * rows, rows)
        acc_ref[pl.ds(r0, rows), :] += x_ref[...]

    @pl.when(k == nk - 1)
    def _():
        xn = acc_ref[...]
        ms = jnp.mean(xn * xn, axis=-1, keepdims=True)
        rs_ref[...] = jnp.broadcast_to(lax.rsqrt(ms + NORM_EPS), rs_ref.shape)

    cw = d // n_cols
    for s in range(n_cols):
        @pl.when(k == nk + s)
        def _(s=s):
            cs = slice(s * cw, (s + 1) * cw)
            xn = acc_ref[:, cs]
            hn = xn * rs_ref[:, 0:1] * g_ref[:, cs]
            if final:
                out_refs[0][...] = hn
            else:
                out_refs[0][...] = xn
                out_refs[1][...] = hn.astype(BF16)


def _proj(prologue, ins, in_specs_fn, w, layer, x, g, *, seq, tm, tk, final):
    m, d = x.shape
    nk = w.shape[-2] // tk
    n_slabs = max(s for s in range(1, min(nk, 8) + 1) if tm % (s * F32_SUBLANES) == 0)
    rows = tm // n_slabs
    n_cols = d // _pick(d, (PROJ_N_CHUNK, LANES))

    def kb(k):
        return jnp.minimum(k, nk - 1)

    kern = functools.partial(_proj_kernel, prologue, len(ins), nk, n_slabs, n_cols, final, tm, seq)
    specs = list(in_specs_fn(kb)) + [
        _w_spec(w, layer, tk, d, lambda i, k: (kb(k), 0)),
        pl.BlockSpec((rows, d), lambda i, k: (i * n_slabs + jnp.minimum(k, n_slabs - 1), 0)),
        pl.BlockSpec((1, d), lambda i, k: (0, 0)),
    ]
    cw = d // n_cols
    slab = pl.BlockSpec((tm, cw), lambda i, k: (i, jnp.clip(k - nk, 0, n_cols - 1)))
    if final:
        out_specs = slab
        out_shape = jax.ShapeDtypeStruct((m, d), F32)
    else:
        out_specs = (slab, slab)
        out_shape = (jax.ShapeDtypeStruct((m, d), F32), jax.ShapeDtypeStruct((m, d), BF16))
    return pl.pallas_call(
        kern,
        grid=(m // tm, nk + n_cols),
        in_specs=specs,
        out_specs=out_specs,
        out_shape=out_shape,
        scratch_shapes=[pltpu.VMEM((tm, d), F32), pltpu.VMEM((tm, LANES), F32)],
        compiler_params=_params(2),
        name="out_proj",
    )(*ins, w, x, g.reshape(1, d))


def _proj_tiles(seq, kdim, tk_max=1024):
    tks = tuple(t for t in (1024, 512, 256, 128) if t <= tk_max)
    return _pick(seq, (1024, 512, 256, 128)), _pick(kdim, tks)


def _proj_plain(a, w, layer, x, g, *, seq):
    tm, tk = _proj_tiles(seq, w.shape[-2])
    specs = lambda kb: [pl.BlockSpec((tm, tk), lambda i, k: (i, kb(k)))]
    return _proj(None, [a], specs, w, layer, x, g, seq=seq, tm=tm, tk=tk, final=False)


def _proj_sc(bcv, conv_w, w, layer, x, g, *, seq):
    m, d = x.shape
    tm, tk = _proj_tiles(seq, d, tk_max=512)
    nb = d // tk

    def specs(kb):
        pos = lambda i, k: (i, kb(k))
        main = lambda off: pl.BlockSpec((tm, tk), lambda i, k: (i, off + kb(k)))
        cp, cn = _halo_specs(tm, tk, nb, m, pos)
        vp, vn = _halo_specs(tm, tk, 2 * nb, m, pos)
        return [main(0), main(nb), main(2 * nb), cp, cn, vp, vn,
                pl.BlockSpec((3, tk), lambda i, k: (0, kb(k)))]

    ins = [bcv, bcv, bcv, bcv, bcv, bcv, bcv, conv_w]
    return _proj(_sc_prologue, ins, specs, w, layer, x, g, seq=seq, tm=tm, tk=tk, final=False)


def _proj_ffn(gv, conv_w, conv_b, w, layer, x, g, *, seq, final):
    m, d = x.shape
    dff = w.shape[-2]
    tm, tk = _proj_tiles(seq, dff, tk_max=512)
    nb = dff // tk

    def specs(kb):
        pos = lambda i, k: (i, kb(k))
        main = lambda off: pl.BlockSpec((tm, tk), lambda i, k: (i, off + kb(k)))
        gp, gn = _halo_specs(tm, tk, 0, m, pos)
        return [main(0), main(nb), gp, gn,
                pl.BlockSpec((3, tk), lambda i, k: (0, kb(k))),
                pl.BlockSpec((1, tk), lambda i, k: (0, kb(k)))]

    ins = [gv, gv, gv, gv, conv_w, conv_b.reshape(1, dff)]
    return _proj(_ffn_prologue, ins, specs, w, layer, x, g, seq=seq, tm=tm, tk=tk, final=final)


def _conv_silu_kernel(tm, seq, x_ref, xp_ref, xn_ref, w_ref, b_ref, o_ref):
    has_prev, has_next = _halo_flags(tm, seq, pl.program_id(0))
    x = x_ref[...].astype(F32)
    x_prev = jnp.where(has_prev, xp_ref[HALO - 1:HALO, :].astype(F32), 0.0)
    x_next = jnp.where(has_next, xn_ref[0:1, :].astype(F32), 0.0)
    y = _conv3_rows(x, x_prev, x_next, w_ref[...]) + b_ref[...]
    o_ref[...] = (y * jax.nn.sigmoid(y)).astype(o_ref.dtype)


def _conv_silu(zx, col_start, conv_w, conv_b, *, seq):
    m = zx.shape[0]
    ncols = conv_w.shape[1]
    tm = _pick(seq, (512, 256, 128))
    tc = _pick(ncols, (2048, 1024, 512, 256, 128))
    assert col_start % tc == 0
    col0 = col_start // tc
    xp, xn = _halo_specs(tm, tc, col0, m, lambda i, c: (i, c))
    return pl.pallas_call(
        functools.partial(_conv_silu_kernel, tm, seq),
        grid=(m // tm, ncols // tc),
        in_specs=[pl.BlockSpec((tm, tc), lambda i, c: (i, col0 + c)), xp, xn,
                  pl.BlockSpec((3, tc), lambda i, c: (0, c)),
                  pl.BlockSpec((1, tc), lambda i, c: (0, c))],
        out_specs=pl.BlockSpec((tm, tc), lambda i, c: (i, c)),
        out_shape=jax.ShapeDtypeStruct((m, ncols), BF16),
        compiler_params=_params(2),
        name="ssd_conv_silu",
    )(zx, zx, zx, conv_w, conv_b.reshape(1, ncols))


def _cumsum_rows(a, reverse):
    n = a.shape[0]
    rid = lax.broadcasted_iota(jnp.int32, a.shape, 0)
    acc = a
    d = 1
    while d < n:
        if reverse:
            acc = acc + jnp.where(rid < n - d, pltpu.roll(acc, n - d, axis=0), 0.0)
        else:
            acc = acc + jnp.where(rid >= d, pltpu.roll(acc, d, axis=0), 0.0)
        d *= 2
    return acc


def _scan_kernel(reverse, n_heads, *refs):
    if reverse:
        (x_ref, bc_ref, dt_ref, dtb_ref, alog_ref, yf_ref, z_ref, ng_ref, o_ref, st_ref) = refs
    else:
        (x_ref, bc_ref, dt_ref, dtb_ref, alog_ref, dskip_ref, o_ref, st_ref) = refs
    L = CHUNK
    hg = n_heads // SSM_GROUPS
    gw = hg * SSM_HEAD_DIM
    pairs = hg // 2
    pw = 2 * SSM_HEAD_DIM

    @pl.when(pl.program_id(1) == 0)
    def _():
        st_ref[...] = jnp.zeros_like(st_ref)

    dt_raw = dt_ref[...] + dtb_ref[...]
    dt = jnp.maximum(dt_raw, 0.0) + jnp.log1p(jnp.exp(-jnp.abs(dt_raw)))
    a = dt * (-jnp.exp(alog_ref[...]))
    acs = _cumsum_rows(a, reverse)
    edge = acs[0:1, :] if reverse else acs[L - 1:L, :]
    eacs = jnp.exp(acs)
    wend = jnp.exp(edge - acs) * dt
    dtot = jnp.exp(edge)
    col2 = acs * LOG2_E
    row2_t = (col2 - jnp.log(dt) * LOG2_E).T
    wend_t = wend.T

    row = lax.broadcasted_iota(jnp.int32, (L, L), 0)
    col = lax.broadcasted_iota(jnp.int32, (L, L), 1)
    keep = (row <= col) if reverse else (row >= col)
    lane = lax.broadcasted_iota(jnp.int32, (L, pw), 1)
    first = lane < SSM_HEAD_DIM
    lane1 = lax.broadcasted_iota(jnp.int32, (1, pw), 1)
    first1 = lane1 < SSM_HEAD_DIM

    for g in range(SSM_GROUPS):
        b_g = bc_ref[:, g * D_STATE:(g + 1) * D_STATE]
        c_g = bc_ref[:, (SSM_GROUPS + g) * D_STATE:(SSM_GROUPS + g + 1) * D_STATE]
        cb = lax.dot_general(c_g, b_g, (((1,), (1,)), ((), ())),
                             preferred_element_type=F32)
        b_t = b_g.astype(F32).T
        z_all = jnp.dot(c_g, st_ref[g].astype(BF16), preferred_element_type=F32)
        ys = []
        for j in range(pairs):
            h0 = g * hg + 2 * j
            h1 = h0 + 1
            c0 = g * gw + j * pw
            xp = x_ref[:, c0:c0 + pw]
            zero = jnp.zeros_like(xp)
            rhs = jnp.concatenate([jnp.where(first, xp, zero),
                                   jnp.where(first, zero, xp)], axis=0)
            atts, bws = [], []
            for h in (h0, h1):
                seg = col2[:, h:h + 1] - row2_t[h:h + 1, :]
                dec = jnp.exp2(jnp.where(keep, seg, -jnp.inf))
                atts.append((cb * dec).astype(BF16))
                bws.append((b_t * wend_t[h:h + 1, :]).astype(BF16))
            y = jnp.dot(jnp.concatenate(atts, axis=1), rhs, preferred_element_type=F32)
            esc = jnp.where(first, eacs[:, h0:h0 + 1], eacs[:, h1:h1 + 1])
            y = y + z_all[:, j * pw:(j + 1) * pw] * esc
            s_old = st_ref[g, :, j * pw:(j + 1) * pw]
            s_dec = jnp.where(first1, dtot[:, h0:h0 + 1], dtot[:, h1:h1 + 1])
            st_ref[g, :, j * pw:(j + 1) * pw] = s_old * s_dec + jnp.dot(
                jnp.concatenate(bws, axis=1), rhs, preferred_element_type=F32)
            if reverse:
                ys.append(y)
            else:
                o_ref[:, c0:c0 + pw] = (y + dskip_ref[:, c0:c0 + pw] * xp.astype(F32)
                                        ).astype(o_ref.dtype)
        if reverse:
            sl = slice(g * gw, (g + 1) * gw)
            zz = z_ref[:, sl].astype(F32)
            u = (jnp.concatenate(ys, axis=1) + yf_ref[:, sl].astype(F32)) * (zz * jax.nn.sigmoid(zz))
            ms = jnp.mean(u * u, axis=-1, keepdims=True)
            o_ref[:, sl] = (u * lax.rsqrt(ms + SSM_NORM_EPS) * ng_ref[:, sl]).astype(o_ref.dtype)


def _ssd_scan(reverse, xbc, dt_raw, dt_bias, a_log, extra, *, batch, seq):
    m = xbc.shape[0]
    n_heads = dt_bias.shape[1]
    d_inner = n_heads * SSM_HEAD_DIM
    bc_w = 2 * SSM_GROUPS * D_STATE
    assert d_inner % bc_w == 0
    nc = seq // CHUNK
    L = CHUNK
    d = 1 if reverse else 0

    def rows(b, c):
        return b * nc + (nc - 1 - c if reverse else c)

    specs = [
        pl.BlockSpec((L, d_inner), lambda b, c: (rows(b, c), 0)),
        pl.BlockSpec((L, bc_w), lambda b, c: (rows(b, c), d_inner // bc_w)),
        pl.BlockSpec((L, n_heads), lambda b, c: (rows(b, c), d)),
        pl.BlockSpec((1, n_heads), lambda b, c: (0, 0)),
        pl.BlockSpec((1, n_heads), lambda b, c: (0, 0)),
    ]
    ins = [xbc, xbc, dt_raw, dt_bias[d:d + 1], a_log[d:d + 1]]
    if reverse:
        yf, zx, norm_g = extra
        specs += [pl.BlockSpec((L, d_inner), lambda b, c: (rows(b, c), 0)),
                  pl.BlockSpec((L, d_inner), lambda b, c: (rows(b, c), 0)),
                  pl.BlockSpec((1, d_inner), lambda b, c: (0, 0))]
        ins += [yf, zx, norm_g.reshape(1, d_inner)]
    else:
        (d_skip,) = extra
        specs += [pl.BlockSpec((1, d_inner), lambda b, c: (0, 0))]
        ins += [jnp.repeat(d_skip, SSM_HEAD_DIM).reshape(1, d_inner)]
    out_dtype = BF16
    gw = d_inner // SSM_GROUPS
    return pl.pallas_call(
        functools.partial(_scan_kernel, reverse, n_heads),
        grid=(batch, nc),
        in_specs=specs,
        out_specs=pl.BlockSpec((L, d_inner), lambda b, c: (rows(b, c), 0)),
        out_shape=jax.ShapeDtypeStruct((m, d_inner), out_dtype),
        scratch_shapes=[pltpu.VMEM((SSM_GROUPS, D_STATE, gw), F32)],
        compiler_params=_params(2),
        name="ssd_scan_bwd" if reverse else "ssd_scan_fwd",
    )(*ins)


def _xattn_kernel(h_ref, x_ref, wq_ref, kv_ref, wo_ref, g_ref, xo_ref, ho_ref):
    width = XA_HEADS * XA_HEAD_DIM
    q = jnp.dot(h_ref[...], wq_ref[...], preferred_element_type=F32).astype(BF16)
    outs = []
    for hd in range(XA_HEADS):
        sl = slice(hd * XA_HEAD_DIM, (hd + 1) * XA_HEAD_DIM)
        k_h = kv_ref[:, sl]
        v_h = kv_ref[:, width + hd * XA_HEAD_DIM: width + (hd + 1) * XA_HEAD_DIM]
        s = lax.dot_general(q[:, sl], k_h, (((1,), (1,)), ((), ())),
                            preferred_element_type=F32) * (XA_HEAD_DIM ** -0.5)
        p = jnp.exp(s - jnp.max(s, axis=-1, keepdims=True))
        denom = jnp.sum(p, axis=-1, keepdims=True)
        o_h = jnp.dot(p.astype(BF16), v_h, preferred_element_type=F32)
        outs.append(o_h / denom)
    o = jnp.concatenate(outs, axis=-1).astype(BF16)
    xn = x_ref[...] + jnp.dot(o, wo_ref[...], preferred_element_type=F32)
    xo_ref[...] = xn
    ms = jnp.mean(xn * xn, axis=-1, keepdims=True)
    ho_ref[...] = (xn * lax.rsqrt(ms + NORM_EPS) * g_ref[...]).astype(BF16)


def _xattn(h, x, wq, kv, wo, layer, g, *, seq):
    m, d = x.shape
    n_mem, kvw = kv.shape[1], kv.shape[2]
    width = XA_HEADS * XA_HEAD_DIM
    tm = _pick(seq, (256, 128))
    row = pl.BlockSpec((tm, d), lambda i: (i, 0))
    return pl.pallas_call(
        _xattn_kernel,
        grid=(m // tm,),
        in_specs=[row, row,
                  _w_spec(wq, layer, d, width, lambda i: (0, 0)),
                  pl.BlockSpec((None, n_mem, kvw), lambda i: ((i * tm) // seq, 0, 0)),
                  _w_spec(wo, layer, width, d, lambda i: (0, 0)),
                  pl.BlockSpec((1, d), lambda i: (0, 0))],
        out_specs=(row, row),
        out_shape=(jax.ShapeDtypeStruct((m, d), F32), jax.ShapeDtypeStruct((m, d), BF16)),
        compiler_params=_params(1),
        name="xattn",
    )(h, x, wq, kv, wo, g.reshape(1, d))


def _trunk(x, mem, p):
    batch, seq, d = x.shape
    m = batch * seq
    n_mem = mem.shape[1]
    depth = p["g_mix"].shape[0]
    d_inner = p["ssm_norm_g"].shape[1]
    xf = x.reshape(m, d)
    memf = mem.reshape(batch * n_mem, d)
    h = _rmsnorm(xf, p["g_mix"][0])
    out = None
    for i in range(depth):
        j = i // 2
        if i % 2 == 0:
            bcv = _matmul(h, p["sc_w_in"], layer=j)
            xf, h = _proj_sc(bcv, p["sc_conv_w"][j], p["sc_w_out"], j, xf, p["g_xattn"][i], seq=seq)
        else:
            w_in = p["ssm_w_in"]
            split = d_inner + p["ssm_conv_w"].shape[2]
            zx = _matmul(h, w_in, layer=j, n=split)
            dt_raw = _matmul(h, w_in, out_dtype=F32, layer=j, col=split, n=w_in.shape[2] - split)
            xbc = _conv_silu(zx, d_inner, p["ssm_conv_w"][j], p["ssm_conv_b"][j], seq=seq)
            yf = _ssd_scan(False, xbc, dt_raw, p["ssm_dt_bias"][j], p["ssm_A_log"][j],
                           (p["ssm_D"][j],), batch=batch, seq=seq)
            a = _ssd_scan(True, xbc, dt_raw, p["ssm_dt_bias"][j], p["ssm_A_log"][j],
                          (yf, zx, p["ssm_norm_g"][j]), batch=batch, seq=seq)
            xf, h = _proj_plain(a, p["ssm_w_out"], j, xf, p["g_xattn"][i], seq=seq)
        kv = _matmul(_rmsnorm(memf, p["g_mem"][i]), p["xa_wkv"], layer=i).reshape(batch, n_mem, -1)
        xf, h = _xattn(h, xf, p["xa_wq"], kv, p["xa_wo"], i, p["g_ffn"][i], seq=seq)
        gv = _matmul(h, p["ffn_w_up"], layer=i)
        if i + 1 < depth:
            xf, h = _proj_ffn(gv, p["ffn_conv_w"][i], p["ffn_conv_b"][i], p["ffn_w_down"], i,
                              xf, p["g_mix"][i + 1], seq=seq, final=False)
        else:
            out = _proj_ffn(gv, p["ffn_conv_w"][i], p["ffn_conv_b"][i], p["ffn_w_down"], i,
                            xf, p["g_final"], seq=seq, final=True)
    return out.reshape(batch, seq, d)


def _pad_to(a, axis, size):
    pad = [(0, 0)] * a.ndim
    pad[axis] = (0, size - a.shape[axis])
    return jnp.pad(a, pad)


def kernel(x_prompt, x_sample, mem_prompt, mem_sample, g_mix, g_xattn, g_mem, g_ffn, g_final, sc_w_in, sc_conv_w, sc_w_out, ssm_w_in, ssm_conv_w, ssm_conv_b, ssm_A_log, ssm_dt_bias, ssm_D, ssm_norm_g, ssm_w_out, xa_wq, xa_wk, xa_wv, xa_wo, ffn_w_up, ffn_conv_w, ffn_conv_b, ffn_w_down):
    d_ff = ffn_w_down.shape[1]
    d_ff_pad = -(-d_ff // 1024) * 1024

    def bf16(w):
        return w.astype(BF16)

    p = {
        "g_mix": g_mix, "g_xattn": g_xattn, "g_mem": g_mem, "g_ffn": g_ffn, "g_final": g_final,
        "sc_w_in": bf16(sc_w_in), "sc_conv_w": sc_conv_w, "sc_w_out": bf16(sc_w_out),
        "ssm_w_in": bf16(ssm_w_in), "ssm_conv_w": ssm_conv_w, "ssm_conv_b": ssm_conv_b,
        "ssm_A_log": ssm_A_log, "ssm_dt_bias": ssm_dt_bias, "ssm_D": ssm_D,
        "ssm_norm_g": ssm_norm_g, "ssm_w_out": bf16(ssm_w_out),
        "xa_wq": bf16(xa_wq),
        "xa_wkv": jnp.concatenate([bf16(xa_wk), bf16(xa_wv)], axis=-1),
        "xa_wo": bf16(xa_wo),
        "ffn_w_up": bf16(_pad_to(ffn_w_up.reshape(ffn_w_up.shape[:2] + (2, d_ff)), 3, d_ff_pad)
                         ).reshape(ffn_w_up.shape[:2] + (2 * d_ff_pad,)),
        "ffn_conv_w": _pad_to(ffn_conv_w, 2, d_ff_pad),
        "ffn_conv_b": _pad_to(ffn_conv_b, 1, d_ff_pad),
        "ffn_w_down": _pad_to(bf16(ffn_w_down), 1, d_ff_pad),
    }
    return (_trunk(x_prompt, mem_prompt, p), _trunk(x_sample, mem_sample, p))
```

```python
import functools

import jax
import jax.numpy as jnp
from jax import lax
from jax.experimental import pallas as pl
from jax.experimental.pallas import tpu as pltpu

F32 = jnp.float32
BF16 = jnp.bfloat16

NORM_EPS = 1e-6
SSM_NORM_EPS = 1e-5
SSM_HEAD_DIM = 64
SSM_GROUPS = 8
D_STATE = 128
CHUNK = 128
XA_HEADS = 4
XA_HEAD_DIM = 128
LOG2_E = 1.4426950408889634

LANES = 128
F32_SUBLANES = 8
BF16_SUBLANES = 16
PROJ_N_CHUNK = 512
MATMUL_N_CHUNK = 1024
HALO = BF16_SUBLANES
V7X_VMEM_BYTES = 64 * 1024 * 1024
VMEM_LIMIT = V7X_VMEM_BYTES - 4 * 1024 * 1024


def _params(n_axes):
    return pltpu.CompilerParams(
        dimension_semantics=("arbitrary",) * n_axes,
        vmem_limit_bytes=VMEM_LIMIT)


def _pick(n, prefs):
    for p in prefs:
        if n % p == 0:
            return p
    return n


def _rmsnorm_kernel(x_ref, g_ref, o_ref):
    x = x_ref[...]
    ms = jnp.mean(x * x, axis=-1, keepdims=True)
    o_ref[...] = (x * lax.rsqrt(ms + NORM_EPS) * g_ref[...]).astype(o_ref.dtype)


def _rmsnorm(x, g):
    m, d = x.shape
    tm = _pick(m, (512, 256, 128))
    return pl.pallas_call(
        _rmsnorm_kernel,
        grid=(m // tm,),
        in_specs=[pl.BlockSpec((tm, d), lambda i: (i, 0)),
                  pl.BlockSpec((1, d), lambda i: (0, 0))],
        out_specs=pl.BlockSpec((tm, d), lambda i: (i, 0)),
        out_shape=jax.ShapeDtypeStruct((m, d), BF16),
        compiler_params=_params(1),
        name="rmsnorm",
    )(x, g.reshape(1, d))


def _matmul_kernel(a_ref, w_ref, o_ref):
    nchunk = _pick(o_ref.shape[1], (MATMUL_N_CHUNK, LANES))
    for n0 in range(0, o_ref.shape[1], nchunk):
        o_ref[:, n0:n0 + nchunk] = jnp.dot(a_ref[...], w_ref[:, n0:n0 + nchunk],
                                           preferred_element_type=F32).astype(o_ref.dtype)


def _w_spec(w, layer, rows, cols, index):
    if w.ndim == 2:
        return pl.BlockSpec((rows, cols), index)
    return pl.BlockSpec((None, rows, cols), lambda *idx: (layer,) + tuple(index(*idx)))


def _matmul(a, w, out_dtype=BF16, layer=None, col=0, n=None):
    m, k = a.shape
    n = w.shape[-1] if n is None else n
    tm = _pick(m, (1024, 512, 256, 128))
    tn = _pick(n, (1024, 512, 256, 128))
    assert col % tn == 0
    c0 = col // tn
    return pl.pallas_call(
        _matmul_kernel,
        grid=(m // tm, n // tn),
        in_specs=[pl.BlockSpec((tm, k), lambda i, j: (i, 0)),
                  _w_spec(w, layer, k, tn, lambda i, j: (0, c0 + j))],
        out_specs=pl.BlockSpec((tm, tn), lambda i, j: (i, j)),
        out_shape=jax.ShapeDtypeStruct((m, n), out_dtype),
        compiler_params=_params(2),
        name="matmul",
    )(a, w)


def _conv3_rows(u, u_prev, u_next, w):
    t = u.shape[0]
    rid = lax.broadcasted_iota(jnp.int32, u.shape, 0)
    below = jnp.where(rid == 0, u_prev, pltpu.roll(u, 1, axis=0))
    above = jnp.where(rid == t - 1, u_next, pltpu.roll(u, t - 1, axis=0))
    return below * w[0:1, :] + u * w[1:2, :] + above * w[2:3, :]


def _halo_flags(tm, seq, i):
    has_prev = (i * tm) % seq != 0
    has_next = ((i + 1) * tm) % seq != 0
    return has_prev, has_next


def _halo_specs(tm, tc, col0, m, pos):
    per = tm // HALO
    last = m // HALO - 1

    def prev(*idx):
        i, c = pos(*idx)
        return jnp.maximum(i * per - 1, 0), col0 + c

    def nxt(*idx):
        i, c = pos(*idx)
        return jnp.minimum((i + 1) * per, last), col0 + c

    return pl.BlockSpec((HALO, tc), prev), pl.BlockSpec((HALO, tc), nxt)


def _sc_prologue(tm, seq, i, refs):
    b_ref, c_ref, v_ref, cp_ref, cn_ref, vp_ref, vn_ref, w_ref = refs
    has_prev, has_next = _halo_flags(tm, seq, i)
    u = c_ref[...].astype(F32) * v_ref[...].astype(F32)
    u_prev = cp_ref[HALO - 1:HALO, :].astype(F32) * vp_ref[HALO - 1:HALO, :].astype(F32)
    u_next = cn_ref[0:1, :].astype(F32) * vn_ref[0:1, :].astype(F32)
    u_prev = jnp.where(has_prev, u_prev, 0.0)
    u_next = jnp.where(has_next, u_next, 0.0)
    y = _conv3_rows(u, u_prev, u_next, w_ref[...])
    return (b_ref[...].astype(F32) * y).astype(BF16)


def _ffn_prologue(tm, seq, i, refs):
    g_ref, v_ref, gp_ref, gn_ref, w_ref, b_ref = refs
    has_prev, has_next = _halo_flags(tm, seq, i)
    g = g_ref[...].astype(F32)
    g_prev = jnp.where(has_prev, gp_ref[HALO - 1:HALO, :].astype(F32), 0.0)
    g_next = jnp.where(has_next, gn_ref[0:1, :].astype(F32), 0.0)
    y = _conv3_rows(g, g_prev, g_next, w_ref[...]) + b_ref[...]
    return (y * jax.nn.sigmoid(y) * v_ref[...].astype(F32)).astype(BF16)


def _proj_kernel(prologue, n_in, nk, n_slabs, final, tm, seq, *refs):
    in_refs = refs[:n_in]
    w_ref, x_ref, g_ref = refs[n_in:n_in + 3]
    out_refs = refs[n_in + 3:]
    acc_ref = out_refs[0]
    i = pl.program_id(0)
    k = pl.program_id(1)
    d = acc_ref.shape[1]
    nchunk = _pick(d, (PROJ_N_CHUNK, LANES))

    def step(first):
        a = in_refs[0][...] if prologue is None else prologue(tm, seq, i, in_refs)
        for n0 in range(0, d, nchunk):
            part = jnp.dot(a, w_ref[:, n0:n0 + nchunk], preferred_element_type=F32)
            if first:
                acc_ref[:, n0:n0 + nchunk] = part
            else:
                acc_ref[:, n0:n0 + nchunk] += part

    pl.when(k == 0)(functools.partial(step, True))
    pl.when(k > 0)(functools.partial(step, False))

    rows = tm // n_slabs

    @pl.when(k < n_slabs)
    def _():
        r0 = pl.multiple_of(k * rows, rows)
        acc_ref[pl.ds(r0, rows), :] += x_ref[...]

    @pl.when(k == nk - 1)
    def _():
        xn = acc_ref[...]
        ms = jnp.mean(xn * xn, axis=-1, keepdims=True)
        hn = xn * lax.rsqrt(ms + NORM_EPS) * g_ref[...]
        if final:
            acc_ref[...] = hn
        else:
            out_refs[1][...] = hn.astype(BF16)


def _proj(prologue, ins, in_specs, w, layer, x, g, *, seq, tm, tk, final):
    m, d = x.shape
    nk = w.shape[-2] // tk
    n_slabs = max(s for s in range(1, min(nk, 8) + 1) if tm % (s * F32_SUBLANES) == 0)
    rows = tm // n_slabs
    kern = functools.partial(_proj_kernel, prologue, len(ins), nk, n_slabs, final, tm, seq)
    specs = list(in_specs) + [
        _w_spec(w, layer, tk, d, lambda i, k: (k, 0)),
        pl.BlockSpec((rows, d), lambda i, k: (i * n_slabs + jnp.minimum(k, n_slabs - 1), 0)),
        pl.BlockSpec((1, d), lambda i, k: (0, 0)),
    ]
    x_spec = pl.BlockSpec((tm, d), lambda i, k: (i, 0))
    if final:
        out_specs = x_spec
        out_shape = jax.ShapeDtypeStruct((m, d), F32)
    else:
        out_specs = (x_spec, pl.BlockSpec((tm, d), lambda i, k: (i, 0)))
        out_shape = (jax.ShapeDtypeStruct((m, d), F32), jax.ShapeDtypeStruct((m, d), BF16))
    return pl.pallas_call(
        kern,
        grid=(m // tm, nk),
        in_specs=specs,
        out_specs=out_specs,
        out_shape=out_shape,
        compiler_params=_params(2),
        name="out_proj",
    )(*ins, w, x, g.reshape(1, d))


def _proj_tiles(seq, kdim, tk_max=1024):
    tks = tuple(t for t in (1024, 512, 256, 128) if t <= tk_max)
    return _pick(seq, (512, 256, 128)), _pick(kdim, tks)


def _proj_plain(a, w, layer, x, g, *, seq):
    tm, tk = _proj_tiles(seq, w.shape[-2])
    specs = [pl.BlockSpec((tm, tk), lambda i, k: (i, k))]
    return _proj(None, [a], specs, w, layer, x, g, seq=seq, tm=tm, tk=tk, final=False)


def _tile_pos(i, k):
    return i, k


def _proj_sc(bcv, conv_w, w, layer, x, g, *, seq):
    m, d = x.shape
    tm, tk = _proj_tiles(seq, d)
    nb = d // tk
    main = lambda off: pl.BlockSpec((tm, tk), lambda i, k: (i, off + k))
    cp, cn = _halo_specs(tm, tk, nb, m, _tile_pos)
    vp, vn = _halo_specs(tm, tk, 2 * nb, m, _tile_pos)
    specs = [main(0), main(nb), main(2 * nb), cp, cn, vp, vn,
             pl.BlockSpec((3, tk), lambda i, k: (0, k))]
    ins = [bcv, bcv, bcv, bcv, bcv, bcv, bcv, conv_w]
    return _proj(_sc_prologue, ins, specs, w, layer, x, g, seq=seq, tm=tm, tk=tk, final=False)


def _proj_ffn(gate, val, conv_w, conv_b, w, layer, x, g, *, seq, final):
    m, d = x.shape
    dff = w.shape[-2]
    tm, tk = _proj_tiles(seq, dff)
    main = pl.BlockSpec((tm, tk), lambda i, k: (i, k))
    gp, gn = _halo_specs(tm, tk, 0, m, _tile_pos)
    specs = [main, main, gp, gn,
             pl.BlockSpec((3, tk), lambda i, k: (0, k)),
             pl.BlockSpec((1, tk), lambda i, k: (0, k))]
    ins = [gate, val, gate, gate, conv_w, conv_b.reshape(1, dff)]
    return _proj(_ffn_prologue, ins, specs, w, layer, x, g, seq=seq, tm=tm, tk=tk, final=final)


def _conv_silu_kernel(tm, seq, x_ref, xp_ref, xn_ref, w_ref, b_ref, o_ref):
    has_prev, has_next = _halo_flags(tm, seq, pl.program_id(0))
    x = x_ref[...].astype(F32)
    x_prev = jnp.where(has_prev, xp_ref[HALO - 1:HALO, :].astype(F32), 0.0)
    x_next = jnp.where(has_next, xn_ref[0:1, :].astype(F32), 0.0)
    y = _conv3_rows(x, x_prev, x_next, w_ref[...]) + b_ref[...]
    o_ref[...] = (y * jax.nn.sigmoid(y)).astype(o_ref.dtype)


def _conv_silu(zx, col_start, conv_w, conv_b, *, seq):
    m = zx.shape[0]
    ncols = conv_w.shape[1]
    tm = _pick(seq, (512, 256, 128))
    tc = _pick(ncols, (2048, 1024, 512, 256, 128))
    assert col_start % tc == 0
    col0 = col_start // tc
    xp, xn = _halo_specs(tm, tc, col0, m, _tile_pos)
    return pl.pallas_call(
        functools.partial(_conv_silu_kernel, tm, seq),
        grid=(m // tm, ncols // tc),
        in_specs=[pl.BlockSpec((tm, tc), lambda i, c: (i, col0 + c)), xp, xn,
                  pl.BlockSpec((3, tc), lambda i, c: (0, c)),
                  pl.BlockSpec((1, tc), lambda i, c: (0, c))],
        out_specs=pl.BlockSpec((tm, tc), lambda i, c: (i, c)),
        out_shape=jax.ShapeDtypeStruct((m, ncols), BF16),
        compiler_params=_params(2),
        name="ssd_conv_silu",
    )(zx, zx, zx, conv_w, conv_b.reshape(1, ncols))


def _cumsum_rows(a, reverse):
    n = a.shape[0]
    rid = lax.broadcasted_iota(jnp.int32, a.shape, 0)
    acc = a
    d = 1
    while d < n:
        if reverse:
            acc = acc + jnp.where(rid < n - d, pltpu.roll(acc, n - d, axis=0), 0.0)
        else:
            acc = acc + jnp.where(rid >= d, pltpu.roll(acc, d, axis=0), 0.0)
        d *= 2
    return acc


def _scan_kernel(reverse, n_heads, *refs):
    if reverse:
        (x_ref, bc_ref, dt_ref, dtb_ref, alog_ref, yf_ref, z_ref, ng_ref, o_ref, st_ref) = refs
    else:
        (x_ref, bc_ref, dt_ref, dtb_ref, alog_ref, dskip_ref, o_ref, st_ref) = refs
    L = CHUNK
    hg = n_heads // SSM_GROUPS
    gw = hg * SSM_HEAD_DIM
    pairs = hg // 2
    pw = 2 * SSM_HEAD_DIM

    @pl.when(pl.program_id(1) == 0)
    def _():
        st_ref[...] = jnp.zeros_like(st_ref)

    dt_raw = dt_ref[...] + dtb_ref[...]
    dt = jnp.maximum(dt_raw, 0.0) + jnp.log1p(jnp.exp(-jnp.abs(dt_raw)))
    a = dt * (-jnp.exp(alog_ref[...]))
    acs = _cumsum_rows(a, reverse)
    edge = acs[0:1, :] if reverse else acs[L - 1:L, :]
    eacs = jnp.exp(acs)
    wend = jnp.exp(edge - acs) * dt
    dtot = jnp.exp(edge)
    col2 = acs * LOG2_E
    row2_t = (col2 - jnp.log(dt) * LOG2_E).T
    wend_t = wend.T

    row = lax.broadcasted_iota(jnp.int32, (L, L), 0)
    col = lax.broadcasted_iota(jnp.int32, (L, L), 1)
    keep = (row <= col) if reverse else (row >= col)
    lane = lax.broadcasted_iota(jnp.int32, (L, pw), 1)
    first = lane < SSM_HEAD_DIM
    lane1 = lax.broadcasted_iota(jnp.int32, (1, pw), 1)
    first1 = lane1 < SSM_HEAD_DIM

    for g in range(SSM_GROUPS):
        b_g = bc_ref[:, g * D_STATE:(g + 1) * D_STATE]
        c_g = bc_ref[:, (SSM_GROUPS + g) * D_STATE:(SSM_GROUPS + g + 1) * D_STATE]
        cb = lax.dot_general(c_g, b_g, (((1,), (1,)), ((), ())),
                             preferred_element_type=F32)
        b_t = b_g.astype(F32).T
        z_all = jnp.dot(c_g, st_ref[g].astype(BF16), preferred_element_type=F32)
        ys = []
        for j in range(pairs):
            h0 = g * hg + 2 * j
            h1 = h0 + 1
            c0 = g * gw + j * pw
            xp = x_ref[:, c0:c0 + pw]
            zero = jnp.zeros_like(xp)
            rhs = jnp.concatenate([jnp.where(first, xp, zero),
                                   jnp.where(first, zero, xp)], axis=0)
            atts, bws = [], []
            for h in (h0, h1):
                seg = col2[:, h:h + 1] - row2_t[h:h + 1, :]
                dec = jnp.exp2(jnp.where(keep, seg, -jnp.inf))
                atts.append((cb * dec).astype(BF16))
                bws.append((b_t * wend_t[h:h + 1, :]).astype(BF16))
            y = jnp.dot(jnp.concatenate(atts, axis=1), rhs, preferred_element_type=F32)
            esc = jnp.take_along_axis(eacs, jnp.where(first, h0, h1).astype(jnp.int32), axis=1,
                                      mode="promise_in_bounds")
            y = y + z_all[:, j * pw:(j + 1) * pw] * esc
            s_old = st_ref[g, :, j * pw:(j + 1) * pw]
            s_dec = jnp.where(first1, dtot[:, h0:h0 + 1], dtot[:, h1:h1 + 1])
            st_ref[g, :, j * pw:(j + 1) * pw] = s_old * s_dec + jnp.dot(
                jnp.concatenate(bws, axis=1), rhs, preferred_element_type=F32)
            if reverse:
                ys.append(y)
            else:
                o_ref[:, c0:c0 + pw] = (y + dskip_ref[:, c0:c0 + pw] * xp.astype(F32)
                                        ).astype(o_ref.dtype)
        if reverse:
            sl = slice(g * gw, (g + 1) * gw)
            zz = z_ref[:, sl].astype(F32)
            u = (jnp.concatenate(ys, axis=1) + yf_ref[:, sl].astype(F32)) * (zz * jax.nn.sigmoid(zz))
            ms = jnp.mean(u * u, axis=-1, keepdims=True)
            o_ref[:, sl] = (u * lax.rsqrt(ms + SSM_NORM_EPS) * ng_ref[:, sl]).astype(o_ref.dtype)


def _ssd_scan(reverse, xbc, dt_raw, dt_bias, a_log, extra, *, batch, seq):
    m = xbc.shape[0]
    n_heads = dt_bias.shape[1]
    d_inner = n_heads * SSM_HEAD_DIM
    bc_w = 2 * SSM_GROUPS * D_STATE
    assert d_inner % bc_w == 0
    nc = seq // CHUNK
    L = CHUNK
    d = 1 if reverse else 0

    def rows(b, c):
        return b * nc + (nc - 1 - c if reverse else c)

    specs = [
        pl.BlockSpec((L, d_inner), lambda b, c: (rows(b, c), 0)),
        pl.BlockSpec((L, bc_w), lambda b, c: (rows(b, c), d_inner // bc_w)),
        pl.BlockSpec((L, n_heads), lambda b, c: (rows(b, c), d)),
        pl.BlockSpec((1, n_heads), lambda b, c: (0, 0)),
        pl.BlockSpec((1, n_heads), lambda b, c: (0, 0)),
    ]
    ins = [xbc, xbc, dt_raw, dt_bias[d:d + 1], a_log[d:d + 1]]
    if reverse:
        yf, zx, norm_g = extra
        specs += [pl.BlockSpec((L, d_inner), lambda b, c: (rows(b, c), 0)),
                  pl.BlockSpec((L, d_inner), lambda b, c: (rows(b, c), 0)),
                  pl.BlockSpec((1, d_inner), lambda b, c: (0, 0))]
        ins += [yf, zx, norm_g.reshape(1, d_inner)]
    else:
        (d_skip,) = extra
        specs += [pl.BlockSpec((1, d_inner), lambda b, c: (0, 0))]
        ins += [jnp.repeat(d_skip, SSM_HEAD_DIM).reshape(1, d_inner)]
    out_dtype = BF16
    gw = d_inner // SSM_GROUPS
    return pl.pallas_call(
        functools.partial(_scan_kernel, reverse, n_heads),
        grid=(batch, nc),
        in_specs=specs,
        out_specs=pl.BlockSpec((L, d_inner), lambda b, c: (rows(b, c), 0)),
        out_shape=jax.ShapeDtypeStruct((m, d_inner), out_dtype),
        scratch_shapes=[pltpu.VMEM((SSM_GROUPS, D_STATE, gw), F32)],
        compiler_params=_params(2),
        name="ssd_scan_bwd" if reverse else "ssd_scan_fwd",
    )(*ins)


def _xattn_kernel(h_ref, x_ref, wq_ref, kv_ref, wo_ref, g_ref, xo_ref, ho_ref):
    width = XA_HEADS * XA_HEAD_DIM
    q = jnp.dot(h_ref[...], wq_ref[...], preferred_element_type=F32).astype(BF16)
    outs = []
    for hd in range(XA_HEADS):
        sl = slice(hd * XA_HEAD_DIM, (hd + 1) * XA_HEAD_DIM)
        k_h = kv_ref[:, sl]
        v_h = kv_ref[:, width + hd * XA_HEAD_DIM: width + (hd + 1) * XA_HEAD_DIM]
        s = lax.dot_general(q[:, sl], k_h, (((1,), (1,)), ((), ())),
                            preferred_element_type=F32) * (XA_HEAD_DIM ** -0.5)
        p = jnp.exp(s - jnp.max(s, axis=-1, keepdims=True))
        denom = jnp.sum(p, axis=-1, keepdims=True)
        o_h = jnp.dot(p.astype(BF16), v_h, preferred_element_type=F32)
        outs.append(o_h / denom)
    o = jnp.concatenate(outs, axis=-1).astype(BF16)
    xn = x_ref[...] + jnp.dot(o, wo_ref[...], preferred_element_type=F32)
    xo_ref[...] = xn
    ms = jnp.mean(xn * xn, axis=-1, keepdims=True)
    ho_ref[...] = (xn * lax.rsqrt(ms + NORM_EPS) * g_ref[...]).astype(BF16)


def _xattn(h, x, wq, kv, wo, layer, g, *, seq):
    m, d = x.shape
    n_mem, kvw = kv.shape[1], kv.shape[2]
    width = XA_HEADS * XA_HEAD_DIM
    tm = _pick(seq, (256, 128))
    row = pl.BlockSpec((tm, d), lambda i: (i, 0))
    return pl.pallas_call(
        _xattn_kernel,
        grid=(m // tm,),
        in_specs=[row, row,
                  _w_spec(wq, layer, d, width, lambda i: (0, 0)),
                  pl.BlockSpec((None, n_mem, kvw), lambda i: ((i * tm) // seq, 0, 0)),
                  _w_spec(wo, layer, width, d, lambda i: (0, 0)),
                  pl.BlockSpec((1, d), lambda i: (0, 0))],
        out_specs=(row, row),
        out_shape=(jax.ShapeDtypeStruct((m, d), F32), jax.ShapeDtypeStruct((m, d), BF16)),
        compiler_params=_params(1),
        name="xattn",
    )(h, x, wq, kv, wo, g.reshape(1, d))


def _trunk(x, mem, p):
    batch, seq, d = x.shape
    m = batch * seq
    n_mem = mem.shape[1]
    depth = p["g_mix"].shape[0]
    d_inner = p["ssm_norm_g"].shape[1]
    xf = x.reshape(m, d)
    memf = mem.reshape(batch * n_mem, d)
    h = _rmsnorm(xf, p["g_mix"][0])
    out = None
    for i in range(depth):
        j = i // 2
        if i % 2 == 0:
            bcv = _matmul(h, p["sc_w_in"], layer=j)
            xf, h = _proj_sc(bcv, p["sc_conv_w"][j], p["sc_w_out"], j, xf, p["g_xattn"][i], seq=seq)
        else:
            w_in = p["ssm_w_in"]
            split = d_inner + p["ssm_conv_w"].shape[2]
            zx = _matmul(h, w_in, layer=j, n=split)
            dt_raw = _matmul(h, w_in, out_dtype=F32, layer=j, col=split, n=w_in.shape[2] - split)
            xbc = _conv_silu(zx, d_inner, p["ssm_conv_w"][j], p["ssm_conv_b"][j], seq=seq)
            yf = _ssd_scan(False, xbc, dt_raw, p["ssm_dt_bias"][j], p["ssm_A_log"][j],
                           (p["ssm_D"][j],), batch=batch, seq=seq)
            a = _ssd_scan(True, xbc, dt_raw, p["ssm_dt_bias"][j], p["ssm_A_log"][j],
                          (yf, zx, p["ssm_norm_g"][j]), batch=batch, seq=seq)
            xf, h = _proj_plain(a, p["ssm_w_out"], j, xf, p["g_xattn"][i], seq=seq)
        kv = _matmul(_rmsnorm(memf, p["g_mem"][i]), p["xa_wkv"], layer=i).reshape(batch, n_mem, -1)
        xf, h = _xattn(h, xf, p["xa_wq"], kv, p["xa_wo"], i, p["g_ffn"][i], seq=seq)
        gate = _matmul(h, p["ffn_w_gate"], layer=i)
        val = _matmul(h, p["ffn_w_val"], layer=i)
        if i + 1 < depth:
            xf, h = _proj_ffn(gate, val, p["ffn_conv_w"][i], p["ffn_conv_b"][i], p["ffn_w_down"], i,
                              xf, p["g_mix"][i + 1], seq=seq, final=False)
        else:
            out = _proj_ffn(gate, val, p["ffn_conv_w"][i], p["ffn_conv_b"][i], p["ffn_w_down"], i,
                            xf, p["g_final"], seq=seq, final=True)
    return out.reshape(batch, seq, d)


def _pad_to(a, axis, size):
    pad = [(0, 0)] * a.ndim
    pad[axis] = (0, size - a.shape[axis])
    return jnp.pad(a, pad)


def kernel(x_prompt, x_sample, mem_prompt, mem_sample, g_mix, g_xattn, g_mem, g_ffn, g_final, sc_w_in, sc_conv_w, sc_w_out, ssm_w_in, ssm_conv_w, ssm_conv_b, ssm_A_log, ssm_dt_bias, ssm_D, ssm_norm_g, ssm_w_out, xa_wq, xa_wk, xa_wv, xa_wo, ffn_w_up, ffn_conv_w, ffn_conv_b, ffn_w_down):
    d_ff = ffn_w_down.shape[1]
    d_ff_pad = -(-d_ff // 1024) * 1024

    def bf16(w):
        return w.astype(BF16)

    p = {
        "g_mix": g_mix, "g_xattn": g_xattn, "g_mem": g_mem, "g_ffn": g_ffn, "g_final": g_final,
        "sc_w_in": bf16(sc_w_in), "sc_conv_w": sc_conv_w, "sc_w_out": bf16(sc_w_out),
        "ssm_w_in": bf16(ssm_w_in), "ssm_conv_w": ssm_conv_w, "ssm_conv_b": ssm_conv_b,
        "ssm_A_log": ssm_A_log, "ssm_dt_bias": ssm_dt_bias, "ssm_D": ssm_D,
        "ssm_norm_g": ssm_norm_g, "ssm_w_out": bf16(ssm_w_out),
        "xa_wq": bf16(xa_wq),
        "xa_wkv": jnp.concatenate([bf16(xa_wk), bf16(xa_wv)], axis=-1),
        "xa_wo": bf16(xa_wo),
        "ffn_w_gate": _pad_to(bf16(ffn_w_up[:, :, :d_ff]), 2, d_ff_pad),
        "ffn_w_val": _pad_to(bf16(ffn_w_up[:, :, d_ff:]), 2, d_ff_pad),
        "ffn_conv_w": _pad_to(ffn_conv_w, 2, d_ff_pad),
        "ffn_conv_b": _pad_to(ffn_conv_b, 1, d_ff_pad),
        "ffn_w_down": _pad_to(bf16(ffn_w_down), 1, d_ff_pad),
    }
    return (_trunk(x_prompt, mem_prompt, p), _trunk(x_sample, mem_sample, p))
```
